```python
import math
import jax, jax.numpy as jnp
from jax import lax
import numpy as np

D_MODEL = 2048
BATCH = 2
SEQ = 16384
DEPTH = 4

CHUNK = 64
QBLOCK = 128
N_MEM = 256
ATTN_WIDTH = D_MODEL // 2
SSM_WIDTH = D_MODEL - ATTN_WIDTH
DIFF_HEAD_DIM = 64
DIFF_V_DIM = 2 * DIFF_HEAD_DIM
DIFF_HEADS = ATTN_WIDTH // DIFF_V_DIM
SSM_GROUP = 16
SSM_GROUPS = SSM_WIDTH // SSM_GROUP
SSM_STATE = 64
XATTN_HEADS = 4
XATTN_HEAD_DIM = D_MODEL // XATTN_HEADS
D_FF = ((8 * D_MODEL // 3 + 255) // 256) * 256
IN_WIDTH = 3 * ATTN_WIDTH + SSM_WIDTH
ALPHA = (2 * DEPTH) ** 0.25
BETA = (8 * DEPTH) ** -0.25
LN_EPS = 1e-5
RMS_EPS = 1e-5
NEG_BIG = -1e30

kernel_name = "hymba_diffattn_s5_macaron_deepnorm"


def layer_norm(x, g, b):
    xf = x.astype(jnp.float32)
    mu = jnp.mean(xf, axis=-1, keepdims=True)
    var = jnp.mean(jnp.square(xf - mu), axis=-1, keepdims=True)
    y = (xf - mu) * lax.rsqrt(var + LN_EPS)
    return (y * g.astype(jnp.float32) + b.astype(jnp.float32)).astype(x.dtype)


def swiglu(x, w_gate, w_up, w_down):
    return (jax.nn.silu(x @ w_gate) * (x @ w_up)) @ w_down


def diff_attention(q, k, v, lam, norm_g, lam_init):
    b_, s_, h_, _, dh = q.shape
    nb = s_ // QBLOCK
    q_blocks = q.reshape(b_, nb, QBLOCK, h_, 2, dh).transpose(1, 0, 2, 3, 4, 5)
    key_chunk = jnp.arange(s_) // CHUNK
    scale = dh ** -0.5

    def one_block(args):
        q_blk, i = args
        s = jnp.einsum('bqhmd,bkhmd->bhmqk', q_blk, k,
                       preferred_element_type=jnp.float32) * scale
        q_chunk = (i * QBLOCK + jnp.arange(QBLOCK)) // CHUNK
        mask = key_chunk[None, :] <= q_chunk[:, None]
        s = jnp.where(mask, s, NEG_BIG)
        p = jax.nn.softmax(s, axis=-1)
        w = p[:, :, 0] - lam * p[:, :, 1]
        return jnp.einsum('bhqk,bkhd->bqhd', w.astype(v.dtype), v)

    o = lax.map(one_block, (q_blocks, jnp.arange(nb)))
    o = o.transpose(1, 0, 2, 3, 4).reshape(b_, s_, h_, v.shape[-1]).astype(jnp.float32)
    o = o * lax.rsqrt(jnp.mean(jnp.square(o), axis=-1, keepdims=True) + RMS_EPS)
    o = o * norm_g.astype(jnp.float32) * (1.0 - lam_init)
    return o.reshape(b_, s_, h_ * v.shape[-1]).astype(v.dtype)


def _linear_recurrence_op(e1, e2):
    a1, b1 = e1
    a2, b2 = e2
    return a1 * a2, a2 * b1 + b2


def s5_mixer(u, lam_re, lam_im, log_step, b_re, b_im, c_re, c_im, d_skip, glu_w, glu_b):
    f32 = jnp.float32
    b_, s_, _ = u.shape
    uf = u.astype(f32)
    lam = lax.complex(jnp.minimum(lam_re.astype(f32), -1e-4), lam_im.astype(f32))
    step = jnp.exp(log_step.astype(f32))[:, None]
    lam_dt = lam * step
    a_bar = jnp.exp(lam_dt)
    b_c = lax.complex(b_re.astype(f32), b_im.astype(f32))
    b_bar = ((a_bar - 1.0) / lam)[..., None] * b_c
    c_c = lax.complex(c_re.astype(f32), c_im.astype(f32))
    a_pow = jnp.exp(lam_dt[None] * jnp.arange(1, CHUNK + 1, dtype=f32)[:, None, None])
    a_seq = jnp.broadcast_to(a_bar, (b_, CHUNK, SSM_GROUPS, SSM_STATE))
    u_chunks = uf.reshape(b_, s_ // CHUNK, CHUNK, SSM_GROUPS, SSM_GROUP).transpose(1, 0, 2, 3, 4)

    def chunk_step(h, u_blk):
        bu = jnp.einsum('blgp,gnp->blgn', u_blk.astype(jnp.complex64), b_bar)
        _, h_loc = lax.associative_scan(_linear_recurrence_op, (a_seq, bu), axis=1)
        h_all = h_loc + a_pow[None] * h[:, None]
        y = jnp.einsum('blgn,gpn->blgp', h_all, c_c).real
        return h_all[:, -1], y

    h0 = jnp.zeros((b_, SSM_GROUPS, SSM_STATE), jnp.complex64)
    _, y = lax.scan(chunk_step, h0, u_chunks)
    y = y.transpose(1, 0, 2, 3, 4).reshape(b_, s_, SSM_WIDTH) + d_skip.astype(f32) * uf
    y = jax.nn.gelu(y)
    y = y * jax.nn.sigmoid(y @ glu_w.astype(f32) + glu_b.astype(f32))
    return y.astype(u.dtype)


def cross_attention(x, mem, w_q, w_k, w_v, w_o):
    b_, s_, _ = x.shape
    m_ = mem.shape[1]
    q = (x @ w_q).reshape(b_, s_, XATTN_HEADS, XATTN_HEAD_DIM)
    k = (mem @ w_k).reshape(b_, m_, XATTN_HEADS, XATTN_HEAD_DIM)
    v = (mem @ w_v).reshape(b_, m_, XATTN_HEADS, XATTN_HEAD_DIM)
    s = jnp.einsum('bshd,bmhd->bhsm', q, k, preferred_element_type=jnp.float32) * XATTN_HEAD_DIM ** -0.5
    p = jax.nn.softmax(s, axis=-1).astype(v.dtype)
    o = jnp.einsum('bhsm,bmhd->bshd', p, v).reshape(b_, s_, D_MODEL)
    return o @ w_o


def setup_inputs(seed: int = 0) -> dict:
    key = jax.random.key(seed)
    ks = list(jax.random.split(key, 48))
    it = iter(ks)
    L, D, F = DEPTH, D_MODEL, D_FF
    G, N, P = SSM_GROUPS, SSM_STATE, SSM_GROUP

    def nrm(shape, scale):
        return jax.random.normal(next(it), shape, jnp.float32) * scale

    def gain(shape):
        return 1.0 + nrm(shape, 0.02)

    col_scale = jnp.concatenate([jnp.ones((2 * ATTN_WIDTH,), jnp.float32),
                                 jnp.full((ATTN_WIDTH,), BETA, jnp.float32),
                                 jnp.ones((SSM_WIDTH,), jnp.float32)])
    inp = {}
    inp["x"] = nrm((BATCH, SEQ, D), 1.0)
    inp["mem"] = nrm((BATCH, N_MEM, D), 1.0)
    inp["ffn1_w_gate"] = nrm((L, D, F), D ** -0.5)
    inp["ffn1_w_up"] = nrm((L, D, F), D ** -0.5)
    inp["ffn1_w_down"] = nrm((L, F, D), BETA * F ** -0.5)
    inp["ln1_g"] = gain((L, D))
    inp["ln1_b"] = nrm((L, D), 0.02)
    inp["w_in"] = nrm((L, D, IN_WIDTH), D ** -0.5) * col_scale
    inp["lambda_q1"] = nrm((L, DIFF_HEAD_DIM), 0.1)
    inp["lambda_k1"] = nrm((L, DIFF_HEAD_DIM), 0.1)
    inp["lambda_q2"] = nrm((L, DIFF_HEAD_DIM), 0.1)
    inp["lambda_k2"] = nrm((L, DIFF_HEAD_DIM), 0.1)
    inp["diff_norm_g"] = gain((L, DIFF_V_DIM))
    inp["ssm_lambda_re"] = -0.5 + nrm((L, G, N), 0.01)
    inp["ssm_lambda_im"] = jnp.pi * jnp.arange(N, dtype=jnp.float32)[None, None, :] + nrm((L, G, N), 0.01)
    inp["ssm_log_step"] = jax.random.uniform(next(it), (L, G), jnp.float32,
                                             minval=math.log(1e-3), maxval=math.log(1e-1))
    inp["ssm_b_re"] = nrm((L, G, N, P), (2 * P) ** -0.5)
    inp["ssm_b_im"] = nrm((L, G, N, P), (2 * P) ** -0.5)
    inp["ssm_c_re"] = nrm((L, G, P, N), (2 * N) ** -0.5)
    inp["ssm_c_im"] = nrm((L, G, P, N), (2 * N) ** -0.5)
    inp["ssm_d"] = nrm((L, SSM_WIDTH), 1.0)
    inp["ssm_glu_w"] = nrm((L, SSM_WIDTH, SSM_WIDTH), SSM_WIDTH ** -0.5)
    inp["ssm_glu_b"] = nrm((L, SSM_WIDTH), 0.02)
    inp["w_out"] = nrm((L, D, D), BETA * D ** -0.5)
    inp["ln2_g"] = gain((L, D))
    inp["ln2_b"] = nrm((L, D), 0.02)
    inp["xattn_w_q"] = nrm((L, D, D), D ** -0.5)
    inp["xattn_w_k"] = nrm((L, D, D), D ** -0.5)
    inp["xattn_w_v"] = nrm((L, D, D), BETA * D ** -0.5)
    inp["xattn_w_o"] = nrm((L, D, D), BETA * D ** -0.5)
    inp["ln3_g"] = gain((L, D))
    inp["ln3_b"] = nrm((L, D), 0.02)
    inp["ffn2_w_gate"] = nrm((L, D, F), D ** -0.5)
    inp["ffn2_w_up"] = nrm((L, D, F), D ** -0.5)
    inp["ffn2_w_down"] = nrm((L, F, D), BETA * F ** -0.5)
    inp["ln4_g"] = gain((L, D))
    inp["ln4_b"] = nrm((L, D), 0.02)
    return inp


def reference(x, mem, ffn1_w_gate, ffn1_w_up, ffn1_w_down, ln1_g, ln1_b, w_in,
              lambda_q1, lambda_k1, lambda_q2, lambda_k2, diff_norm_g,
              ssm_lambda_re, ssm_lambda_im, ssm_log_step, ssm_b_re, ssm_b_im, ssm_c_re, ssm_c_im,
              ssm_d, ssm_glu_w, ssm_glu_b, w_out, ln2_g, ln2_b,
              xattn_w_q, xattn_w_k, xattn_w_v, xattn_w_o, ln3_g, ln3_b,
              ffn2_w_gate, ffn2_w_up, ffn2_w_down, ln4_g, ln4_b):
    b_, s_, _ = x.shape
    for l in range(DEPTH):
        lam_init = 0.8 - 0.6 * math.exp(-0.3 * l)
        x = layer_norm(ALPHA * x + 0.5 * swiglu(x, ffn1_w_gate[l], ffn1_w_up[l], ffn1_w_down[l]),
                       ln1_g[l], ln1_b[l])
        h = x @ w_in[l]
        q = h[..., :ATTN_WIDTH].reshape(b_, s_, DIFF_HEADS, 2, DIFF_HEAD_DIM)
        k = h[..., ATTN_WIDTH:2 * ATTN_WIDTH].reshape(b_, s_, DIFF_HEADS, 2, DIFF_HEAD_DIM)
        v = h[..., 2 * ATTN_WIDTH:3 * ATTN_WIDTH].reshape(b_, s_, DIFF_HEADS, DIFF_V_DIM)
        u = h[..., 3 * ATTN_WIDTH:]
        lam = (jnp.exp(jnp.sum(lambda_q1[l].astype(jnp.float32) * lambda_k1[l].astype(jnp.float32)))
               - jnp.exp(jnp.sum(lambda_q2[l].astype(jnp.float32) * lambda_k2[l].astype(jnp.float32)))
               + lam_init)
        attn_out = diff_attention(q, k, v, lam, diff_norm_g[l], lam_init)
        ssm_out = s5_mixer(u, ssm_lambda_re[l], ssm_lambda_im[l], ssm_log_step[l],
                           ssm_b_re[l], ssm_b_im[l], ssm_c_re[l], ssm_c_im[l],
                           ssm_d[l], ssm_glu_w[l], ssm_glu_b[l])
        mixed = jnp.concatenate([attn_out, ssm_out], axis=-1) @ w_out[l]
        x = layer_norm(ALPHA * x + mixed, ln2_g[l], ln2_b[l])
        x = layer_norm(ALPHA * x + cross_attention(x, mem, xattn_w_q[l], xattn_w_k[l],
                                                     xattn_w_v[l], xattn_w_o[l]),
                       ln3_g[l], ln3_b[l])
        x = layer_norm(ALPHA * x + 0.5 * swiglu(x, ffn2_w_gate[l], ffn2_w_up[l], ffn2_w_down[l]),
                       ln4_g[l], ln4_b[l])
    return x
```

```python
import functools
import math

import jax
import jax.numpy as jnp
from jax import lax
from jax.experimental import pallas as pl
from jax.experimental.pallas import tpu as pltpu

F32 = jnp.float32
BF16 = jnp.bfloat16

D_MODEL = 2048
DEPTH = 4
CHUNK = 64
ATTN_WIDTH = D_MODEL // 2
SSM_WIDTH = D_MODEL - ATTN_WIDTH
DIFF_HEAD_DIM = 64
DIFF_V_DIM = 2 * DIFF_HEAD_DIM
DIFF_HEADS = ATTN_WIDTH // DIFF_V_DIM
SSM_GROUP = 16
SSM_GROUPS = SSM_WIDTH // SSM_GROUP
SSM_STATE = 64
XATTN_HEADS = 4
XATTN_HEAD_DIM = D_MODEL // XATTN_HEADS
ALPHA = (2 * DEPTH) ** 0.25
LN_EPS = 1e-5
RMS_EPS = 1e-5
NEG_BIG = -1e30

V7X_VMEM_LIMIT_BYTES = 56 * 1024 * 1024

FFN_TM = 512
FFN_TF = 512
PROJ_TM = 512
ATTN_TQ = 512
ATTN_TK = 512
S5_CHUNK = 64


def _params(*sem):
    return pltpu.CompilerParams(dimension_semantics=sem,
                                vmem_limit_bytes=V7X_VMEM_LIMIT_BYTES)


def _resident(shape):
    return pl.BlockSpec(shape, lambda *_: (0,) * len(shape), pipeline_mode=pl.Buffered(1))


def _layer_norm(y, g, b):
    mu = jnp.mean(y, axis=-1, keepdims=True)
    yc = y - mu
    var = jnp.mean(yc * yc, axis=-1, keepdims=True)
    return yc * lax.rsqrt(var + LN_EPS) * g + b


def _ffn_ln_kernel(x_ref, wg_ref, wu_ref, wd_ref, g_ref, b_ref, o_ref, xb_ref, acc_ref):
    f = pl.program_id(1)

    @pl.when(f == 0)
    def _():
        xb_ref[...] = x_ref[...].astype(BF16)
        acc_ref[...] = jnp.zeros_like(acc_ref)

    xb = xb_ref[...]
    gate = jnp.dot(xb, wg_ref[...], preferred_element_type=F32)
    up = jnp.dot(xb, wu_ref[...], preferred_element_type=F32)
    h = (gate * jax.nn.sigmoid(gate) * up).astype(BF16)
    acc_ref[...] += jnp.dot(h, wd_ref[...], preferred_element_type=F32)

    @pl.when(f == pl.num_programs(1) - 1)
    def _():
        y = ALPHA * x_ref[...] + 0.5 * acc_ref[...]
        o_ref[...] = _layer_norm(y, g_ref[...], b_ref[...])


def _ffn_ln(x, wg, wu, wd, g, b):
    t, d = x.shape
    f = wg.shape[1]
    tm, tf = min(FFN_TM, t), FFN_TF
    return pl.pallas_call(
        _ffn_ln_kernel,
        grid=(t // tm, f // tf),
        in_specs=[
            pl.BlockSpec((tm, d), lambda i, j: (i, 0)),
            pl.BlockSpec((d, tf), lambda i, j: (0, j)),
            pl.BlockSpec((d, tf), lambda i, j: (0, j)),
            pl.BlockSpec((tf, d), lambda i, j: (j, 0)),
            pl.BlockSpec((1, d), lambda i, j: (0, 0)),
            pl.BlockSpec((1, d), lambda i, j: (0, 0)),
        ],
        out_specs=pl.BlockSpec((tm, d), lambda i, j: (i, 0)),
        out_shape=jax.ShapeDtypeStruct((t, d), F32),
        scratch_shapes=[pltpu.VMEM((tm, d), BF16), pltpu.VMEM((tm, d), F32)],
        compiler_params=_params("parallel", "arbitrary"),
        name="ffn_ln",
    )(x, wg, wu, wd, g, b)


def _in_proj_kernel(x_ref, wku_ref, wqvt_ref, ku_ref, qvt_ref):
    xb = x_ref[...].astype(BF16)
    ku_ref[...] = jnp.dot(xb, wku_ref[...], preferred_element_type=F32).astype(BF16)
    qvt = lax.dot_general(wqvt_ref[...], xb, (((1,), (1,)), ((), ())),
                          preferred_element_type=F32)
    qvt_ref[...] = qvt.astype(BF16)


def _in_proj(x, w_ku, w_qv_t):
    t, d = x.shape
    tm = min(PROJ_TM, t)
    n_ku, n_qv = w_ku.shape[1], w_qv_t.shape[0]
    return pl.pallas_call(
        _in_proj_kernel,
        grid=(t // tm,),
        in_specs=[
            pl.BlockSpec((tm, d), lambda i: (i, 0)),
            _resident((d, n_ku)),
            _resident((n_qv, d)),
        ],
        out_specs=[
            pl.BlockSpec((tm, n_ku), lambda i: (i, 0)),
            pl.BlockSpec((n_qv, tm), lambda i: (0, i)),
        ],
        out_shape=[jax.ShapeDtypeStruct((t, n_ku), BF16),
                   jax.ShapeDtypeStruct((n_qv, t), BF16)],
        compiler_params=_params("parallel"),
        name="in_proj",
    )(x, w_ku, w_qv_t)


def _diff_attn_kernel(qt_ref, k_ref, vt_ref, lam_ref, g_ref, o_ref,
                      acc_ref, m_ref, l_ref, *, tq, tk, lam_init):
    i = pl.program_id(2)
    dh = DIFF_HEAD_DIM

    qt = qt_ref[...]
    row = lax.broadcasted_iota(jnp.int32, qt.shape, 0)
    zero = jnp.zeros_like(qt)
    scale = jnp.asarray(dh ** -0.5, BF16)
    qz = jnp.concatenate([jnp.where(row < dh, qt, zero),
                          jnp.where(row >= dh, qt, zero)], axis=1) * scale

    m_ref[...] = jnp.full(m_ref.shape, NEG_BIG, F32)
    l_ref[...] = jnp.zeros_like(l_ref)
    acc_ref[...] = jnp.zeros_like(acc_ref)

    def step(j, masked):
        k0 = pl.multiple_of(j * tk, tk)
        kb = k_ref[pl.ds(k0, tk), :]
        s = jnp.dot(kb, qz, preferred_element_type=F32)
        if masked:
            kc = lax.broadcasted_iota(jnp.int32, s.shape, 0) // CHUNK
            qc = (lax.broadcasted_iota(jnp.int32, s.shape, 1) % tq) // CHUNK
            s = jnp.where(kc <= qc, s, NEG_BIG)
        m_old = m_ref[...]
        m_new = jnp.maximum(m_old, jnp.max(s, axis=0, keepdims=True))
        corr = jnp.exp(m_old - m_new)
        p = jnp.exp(s - m_new)
        l_ref[...] = corr * l_ref[...] + jnp.sum(p, axis=0, keepdims=True)
        vtb = vt_ref[:, pl.ds(k0, tk)]
        acc_ref[...] = corr * acc_ref[...] + jnp.dot(vtb, p.astype(BF16),
                                                     preferred_element_type=F32)
        m_ref[...] = m_new

    def body(j, carry):
        step(j, False)
        return carry

    lax.fori_loop(0, i, body, 0)
    step(i, True)

    lam_p = lam_ref[...]
    lam = (jnp.exp(jnp.sum(lam_p[0:1] * lam_p[1:2], axis=-1, keepdims=True))
           - jnp.exp(jnp.sum(lam_p[2:3] * lam_p[3:4], axis=-1, keepdims=True))
           + lam_init)
    acc = acc_ref[...]
    l = l_ref[...]
    ot = acc[:, :tq] / l[:, :tq] - lam * (acc[:, tq:] / l[:, tq:])
    o = ot.T
    o = o * lax.rsqrt(jnp.mean(o * o, axis=-1, keepdims=True) + RMS_EPS)
    o_ref[...] = (o * g_ref[...] * (1.0 - lam_init)).astype(o_ref.dtype)


def _diff_attn(qvt, ku, lam_p, g, lam_init, batch, seq):
    tq, tk = min(ATTN_TQ, seq), min(ATTN_TK, seq)
    assert tq == tk and tq % CHUNK == 0
    nq = seq // tq
    hh = DIFF_HEADS
    dv = DIFF_V_DIM
    kern = functools.partial(_diff_attn_kernel, tq=tq, tk=tk, lam_init=lam_init)
    return pl.pallas_call(
        kern,
        grid=(batch, hh, nq),
        in_specs=[
            pl.BlockSpec((dv, tq), lambda b, h, i: (h, b * nq + i)),
            pl.BlockSpec((seq, dv), lambda b, h, i: (b, h)),
            pl.BlockSpec((dv, seq), lambda b, h, i: (hh + h, b)),
            pl.BlockSpec((4, DIFF_HEAD_DIM), lambda b, h, i: (0, 0)),
            pl.BlockSpec((1, dv), lambda b, h, i: (0, 0)),
        ],
        out_specs=pl.BlockSpec((tq, dv), lambda b, h, i: (b * nq + i, h)),
        out_shape=jax.ShapeDtypeStruct((batch * seq, hh * dv), BF16),
        scratch_shapes=[pltpu.VMEM((dv, 2 * tq), F32),
                        pltpu.VMEM((1, 2 * tq), F32),
                        pltpu.VMEM((1, 2 * tq), F32)],
        compiler_params=_params("parallel", "parallel", "arbitrary"),
        name="diff_attn",
    )(qvt, ku, qvt, lam_p, g)


def _s5_kernel(u_ref, m_ref, gre_ref, gim_ref, cr_ref, ci_ref, al_ref, y_ref,
               pre_ref, pim_ref, sre_ref, sim_ref, *, batch):
    u = u_ref[0]
    rows = u.shape[0]
    pre_ref[...] = jnp.dot(u, gre_ref[0], preferred_element_type=F32)
    pim_ref[...] = jnp.dot(u, gim_ref[0], preferred_element_type=F32)
    a_re = al_ref[0, 0:1, :]
    a_im = al_ref[0, 1:2, :]

    tile = 8
    assert tile % batch == 0 and rows % tile == 0

    def body(r, carry):
        s_re, s_im = carry
        r0 = pl.multiple_of(r * tile, tile)
        p_re = pre_ref[pl.ds(r0, tile), :]
        p_im = pim_ref[pl.ds(r0, tile), :]
        prev_re, prev_im = [], []
        for c in range(tile // batch):
            prev_re.append(s_re)
            prev_im.append(s_im)
            sl = slice(c * batch, (c + 1) * batch)
            s_re, s_im = (a_re * s_re - a_im * s_im + p_re[sl],
                          a_re * s_im + a_im * s_re + p_im[sl])
        sre_ref[pl.ds(r0, tile), :] = jnp.concatenate(prev_re, axis=0)
        sim_ref[pl.ds(r0, tile), :] = jnp.concatenate(prev_im, axis=0)
        return s_re, s_im

    zero = jnp.zeros((batch, SSM_STATE), F32)
    lax.fori_loop(0, rows // tile, body, (zero, zero))

    def split_dot(s, w):
        hi = s.astype(BF16)
        lo = (s - hi.astype(F32)).astype(BF16)
        return (jnp.dot(hi, w, preferred_element_type=F32)
                + jnp.dot(lo, w, preferred_element_type=F32))

    y = jnp.dot(u, m_ref[0], preferred_element_type=F32)
    y = y + split_dot(sre_ref[...], cr_ref[0]) - split_dot(sim_ref[...], ci_ref[0])
    y_ref[0] = y


def _s5_conv(u_g, toep, g_re, g_im, c_re, c_im, a_l, batch):
    groups, rows, lp = u_g.shape
    n = SSM_STATE
    kern = functools.partial(_s5_kernel, batch=batch)
    return pl.pallas_call(
        kern,
        grid=(groups,),
        in_specs=[
            pl.BlockSpec((1, rows, lp), lambda g: (g, 0, 0)),
            pl.BlockSpec((1, lp, lp), lambda g: (g, 0, 0)),
            pl.BlockSpec((1, lp, n), lambda g: (g, 0, 0)),
            pl.BlockSpec((1, lp, n), lambda g: (g, 0, 0)),
            pl.BlockSpec((1, n, lp), lambda g: (g, 0, 0)),
            pl.BlockSpec((1, n, lp), lambda g: (g, 0, 0)),
            pl.BlockSpec((1, 2, n), lambda g: (g, 0, 0)),
        ],
        out_specs=pl.BlockSpec((1, rows, lp), lambda g: (g, 0, 0)),
        out_shape=jax.ShapeDtypeStruct((groups, rows, lp), F32),
        scratch_shapes=[pltpu.VMEM((rows, n), F32)] * 4,
        compiler_params=_params("parallel"),
        name="s5_conv",
    )(u_g, toep, g_re, g_im, c_re, c_im, a_l)


def _s5_operators(lam_re, lam_im, log_step, b_re, b_im, c_re, c_im):
    hp = lax.Precision.HIGHEST
    L, P = S5_CHUNK, SSM_GROUP
    lr = jnp.minimum(lam_re.astype(F32), -1e-4)
    li = lam_im.astype(F32)
    step = jnp.exp(log_step.astype(F32))[:, None]
    dr, di = lr * step, li * step
    tau = jnp.arange(L + 1, dtype=F32)[:, None, None]
    mag = jnp.exp(tau * dr)
    pw_re, pw_im = mag * jnp.cos(tau * di), mag * jnp.sin(tau * di)
    a_re, a_im = pw_re[1], pw_im[1]
    den = lr * lr + li * li
    f_re = ((a_re - 1.0) * lr + a_im * li) / den
    f_im = (a_im * lr - (a_re - 1.0) * li) / den
    br, bi = b_re.astype(F32), b_im.astype(F32)
    bb_re = f_re[..., None] * br - f_im[..., None] * bi
    bb_im = f_re[..., None] * bi + f_im[..., None] * br
    cr, ci = c_re.astype(F32), c_im.astype(F32)
    ca_re = cr[None] * pw_re[:, :, None, :] - ci[None] * pw_im[:, :, None, :]
    ca_im = cr[None] * pw_im[:, :, None, :] + ci[None] * pw_re[:, :, None, :]
    taps = (jnp.einsum('tgpn,gnq->tgpq', ca_re[:L], bb_re, precision=hp)
            - jnp.einsum('tgpn,gnq->tgpq', ca_im[:L], bb_im, precision=hp))
    jj = jnp.arange(L)[:, None]
    ll = jnp.arange(L)[None, :]
    idx = jnp.clip(ll - jj, 0, L - 1)
    toep = jnp.where((ll >= jj)[:, :, None, None, None], taps[idx], 0.0)
    toep = toep.transpose(2, 0, 4, 1, 3).reshape(-1, L * P, L * P)
    rev_re, rev_im = pw_re[L - 1::-1][:L], pw_im[L - 1::-1][:L]
    gm_re = rev_re[..., None] * bb_re[None] - rev_im[..., None] * bb_im[None]
    gm_im = rev_re[..., None] * bb_im[None] + rev_im[..., None] * bb_re[None]
    gm_re = gm_re.transpose(1, 0, 3, 2).reshape(-1, L * P, SSM_STATE)
    gm_im = gm_im.transpose(1, 0, 3, 2).reshape(-1, L * P, SSM_STATE)
    co_re = ca_re[1:].transpose(1, 3, 0, 2).reshape(-1, SSM_STATE, L * P)
    co_im = ca_im[1:].transpose(1, 3, 0, 2).reshape(-1, SSM_STATE, L * P)
    a_l = jnp.stack([pw_re[L], pw_im[L]], axis=1)
    return (toep.astype(BF16), gm_re.astype(BF16), gm_im.astype(BF16),
            co_re.astype(BF16), co_im.astype(BF16), a_l)


def _mix_out_kernel(x_ref, attn_ref, y_ref, u_ref, d_ref, gw_ref, gb_ref, wo_ref,
                    g_ref, b_ref, o_ref):
    y = y_ref[...] + d_ref[...] * u_ref[...].astype(F32)
    y = jax.nn.gelu(y)
    z = jnp.dot(y.astype(BF16), gw_ref[...], preferred_element_type=F32) + gb_ref[...]
    y = y * jax.nn.sigmoid(z)
    aw = attn_ref.shape[1]
    mixed = (jnp.dot(attn_ref[...], wo_ref[:aw, :], preferred_element_type=F32)
             + jnp.dot(y.astype(BF16), wo_ref[aw:, :], preferred_element_type=F32))
    o_ref[...] = _layer_norm(ALPHA * x_ref[...] + mixed, g_ref[...], b_ref[...])


def _mix_out(x, attn, y_ssm, ku, d_skip, glu_w, glu_b, w_out, g, b):
    t, d = x.shape
    tm = min(PROJ_TM, t)
    aw, sw = attn.shape[1], y_ssm.shape[1]
    row = lambda i: (i, 0)
    return pl.pallas_call(
        _mix_out_kernel,
        grid=(t // tm,),
        in_specs=[
            pl.BlockSpec((tm, d), row),
            pl.BlockSpec((tm, aw), row),
            pl.BlockSpec((tm, sw), row),
            pl.BlockSpec((tm, sw), lambda i: (i, 1)),
            _resident((1, sw)),
            _resident((sw, sw)),
            _resident((1, sw)),
            _resident((d, d)),
            _resident((1, d)),
            _resident((1, d)),
        ],
        out_specs=pl.BlockSpec((tm, d), row),
        out_shape=jax.ShapeDtypeStruct((t, d), F32),
        compiler_params=_params("parallel"),
        name="mix_out",
    )(x, attn, y_ssm, ku, d_skip, glu_w, glu_b, w_out, g, b)


def _mem_proj_kernel(m_ref, wk_ref, wv_ref, k_ref, v_ref):
    mb = m_ref[...].astype(BF16)
    k_ref[...] = jnp.dot(mb, wk_ref[...], preferred_element_type=F32).astype(BF16)
    v_ref[...] = jnp.dot(mb, wv_ref[...], preferred_element_type=F32).astype(BF16)


def _mem_proj(mem2d, wk, wv):
    rows, d = mem2d.shape
    tr = min(256, rows)
    return pl.pallas_call(
        _mem_proj_kernel,
        grid=(rows // tr,),
        in_specs=[pl.BlockSpec((tr, d), lambda i: (i, 0)),
                  _resident((d, d)),
                  _resident((d, d))],
        out_specs=[pl.BlockSpec((tr, d), lambda i: (i, 0)),
                   pl.BlockSpec((tr, d), lambda i: (i, 0))],
        out_shape=[jax.ShapeDtypeStruct((rows, d), BF16)] * 2,
        compiler_params=_params("parallel"),
        name="mem_proj",
    )(mem2d, wk, wv)


def _xattn_ln_kernel(x_ref, wq_ref, k_ref, v_ref, wo_ref, g_ref, b_ref, o_ref, oc_ref):
    x = x_ref[...]
    q = jnp.dot(x.astype(BF16), wq_ref[...], preferred_element_type=F32).astype(BF16)
    hd = XATTN_HEAD_DIM
    for h in range(XATTN_HEADS):
        sl = slice(h * hd, (h + 1) * hd)
        s = lax.dot_general(q[:, sl], k_ref[0, :, sl], (((1,), (1,)), ((), ())),
                            preferred_element_type=F32) * (hd ** -0.5)
        s = s - jnp.max(s, axis=-1, keepdims=True)
        e = jnp.exp(s)
        p = e / jnp.sum(e, axis=-1, keepdims=True)
        oc_ref[:, sl] = jnp.dot(p.astype(BF16), v_ref[0, :, sl],
                                preferred_element_type=F32).astype(BF16)
    out = jnp.dot(oc_ref[...], wo_ref[...], preferred_element_type=F32)
    o_ref[...] = _layer_norm(ALPHA * x + out, g_ref[...], b_ref[...])


def _xattn_ln(x, wq, k_mem, v_mem, wo, g, b, seq):
    t, d = x.shape
    tm = min(PROJ_TM, seq)
    per_b = seq // tm
    n_mem = k_mem.shape[1]
    return pl.pallas_call(
        _xattn_ln_kernel,
        grid=(t // tm,),
        in_specs=[
            pl.BlockSpec((tm, d), lambda i: (i, 0)),
            _resident((d, d)),
            pl.BlockSpec((1, n_mem, d), lambda i: (i // per_b, 0, 0)),
            pl.BlockSpec((1, n_mem, d), lambda i: (i // per_b, 0, 0)),
            _resident((d, d)),
            _resident((1, d)),
            _resident((1, d)),
        ],
        out_specs=pl.BlockSpec((tm, d), lambda i: (i, 0)),
        out_shape=jax.ShapeDtypeStruct((t, d), F32),
        scratch_shapes=[pltpu.VMEM((tm, d), BF16)],
        compiler_params=_params("parallel"),
        name="xattn_ln",
    )(x, wq, k_mem, v_mem, wo, g, b)


def kernel(x, mem, ffn1_w_gate, ffn1_w_up, ffn1_w_down, ln1_g, ln1_b, w_in, lambda_q1, lambda_k1, lambda_q2, lambda_k2, diff_norm_g, ssm_lambda_re, ssm_lambda_im, ssm_log_step, ssm_b_re, ssm_b_im, ssm_c_re, ssm_c_im, ssm_d, ssm_glu_w, ssm_glu_b, w_out, ln2_g, ln2_b, xattn_w_q, xattn_w_k, xattn_w_v, xattn_w_o, ln3_g, ln3_b, ffn2_w_gate, ffn2_w_up, ffn2_w_down, ln4_g, ln4_b):
    batch, seq, d = x.shape
    t = batch * seq
    n_mem = mem.shape[1]
    aw = ATTN_WIDTH
    L, P, G = S5_CHUNK, SSM_GROUP, SSM_GROUPS
    n_chunks = seq // L
    xf = x.reshape(t, d)
    mem2d = mem.reshape(batch * n_mem, d)
    row = lambda a: a.reshape(1, -1).astype(F32)

    for l in range(DEPTH):
        lam_init = 0.8 - 0.6 * math.exp(-0.3 * l)
        xf = _ffn_ln(xf, ffn1_w_gate[l].astype(BF16), ffn1_w_up[l].astype(BF16),
                     ffn1_w_down[l].astype(BF16), row(ln1_g[l]), row(ln1_b[l]))

        w = w_in[l]
        w_ku = jnp.concatenate([w[:, aw:2 * aw], w[:, 3 * aw:]], axis=1).astype(BF16)
        w_qv_t = jnp.concatenate([w[:, :aw], w[:, 2 * aw:3 * aw]], axis=1).T.astype(BF16)
        ku, qvt = _in_proj(xf, w_ku, w_qv_t)

        lam_p = jnp.stack([lambda_q1[l], lambda_k1[l], lambda_q2[l], lambda_k2[l]]).astype(F32)
        attn = _diff_attn(qvt, ku, lam_p, row(diff_norm_g[l]), lam_init, batch, seq)

        ops = _s5_operators(ssm_lambda_re[l], ssm_lambda_im[l], ssm_log_step[l],
                            ssm_b_re[l], ssm_b_im[l], ssm_c_re[l], ssm_c_im[l])
        u_g = (ku[:, aw:].reshape(batch, n_chunks, L, G, P)
               .transpose(3, 1, 0, 2, 4).reshape(G, n_chunks * batch, L * P))
        y_g = _s5_conv(u_g, *ops, batch)
        y_ssm = (y_g.reshape(G, n_chunks, batch, L, P)
                 .transpose(2, 1, 3, 0, 4).reshape(t, G * P))

        xf = _mix_out(xf, attn, y_ssm, ku, row(ssm_d[l]), ssm_glu_w[l].astype(BF16),
                      row(ssm_glu_b[l]), w_out[l].astype(BF16), row(ln2_g[l]), row(ln2_b[l]))

        k_mem, v_mem = _mem_proj(mem2d, xattn_w_k[l].astype(BF16), xattn_w_v[l].astype(BF16))
        xf = _xattn_ln(xf, xattn_w_q[l].astype(BF16), k_mem.reshape(batch, n_mem, d),
                       v_mem.reshape(batch, n_mem, d), xattn_w_o[l].astype(BF16),
                       row(ln3_g[l]), row(ln3_b[l]), seq)

        xf = _ffn_ln(xf, ffn2_w_gate[l].astype(BF16), ffn2_w_up[l].astype(BF16),
                     ffn2_w_down[l].astype(BF16), row(ln4_g[l]), row(ln4_b[l]))
    return xf.reshape(batch, seq, d)
```

```python
import functools
import math

import jax
import jax.numpy as jnp
from jax import lax
from jax.experimental import pallas as pl
from jax.experimental.pallas import tpu as pltpu

F32 = jnp.float32
BF16 = jnp.bfloat16

D_MODEL = 2048
DEPTH = 4
CHUNK = 64
ATTN_WIDTH = D_MODEL // 2
SSM_WIDTH = D_MODEL - ATTN_WIDTH
DIFF_HEAD_DIM = 64
DIFF_V_DIM = 2 * DIFF_HEAD_DIM
DIFF_HEADS = ATTN_WIDTH // DIFF_V_DIM
SSM_GROUP = 16
SSM_GROUPS = SSM_WIDTH // SSM_GROUP
SSM_STATE = 64
XATTN_HEADS = 4
XATTN_HEAD_DIM = D_MODEL // XATTN_HEADS
ALPHA = (2 * DEPTH) ** 0.25
LN_EPS = 1e-5
RMS_EPS = 1e-5
NEG_BIG = -1e30
LOG2_E = 1.4426950408889634

V7X_VMEM_LIMIT_BYTES = 56 * 1024 * 1024

FFN_TM = 512
FFN_TF = 512
PROJ_TM = 512
ATTN_TQ = 512
ATTN_TK = 512
ATTN_KS = 256
ATTN_QS = 512
ATTN_LOOKAHEAD = 2
ATTN_SHIFT_SLACK = 64.0
S5_CHUNK = 64


def _params(*sem):
    return pltpu.CompilerParams(dimension_semantics=sem,
                                vmem_limit_bytes=V7X_VMEM_LIMIT_BYTES)


def _resident(shape):
    return pl.BlockSpec(shape, lambda *_: (0,) * len(shape), pipeline_mode=pl.Buffered(1))


def _layer_norm(y, g, b):
    mu = jnp.mean(y, axis=-1, keepdims=True)
    yc = y - mu
    var = jnp.mean(yc * yc, axis=-1, keepdims=True)
    return yc * lax.rsqrt(var + LN_EPS) * g + b


def _ffn_ln_kernel(x_ref, wg_ref, wu_ref, wd_ref, g_ref, b_ref, o_ref, xb_ref, acc_ref):
    f = pl.program_id(1)

    @pl.when(f == 0)
    def _():
        xb_ref[...] = x_ref[...].astype(BF16)
        acc_ref[...] = jnp.zeros_like(acc_ref)

    xb = xb_ref[...]
    gate = jnp.dot(xb, wg_ref[...], preferred_element_type=F32)
    up = jnp.dot(xb, wu_ref[...], preferred_element_type=F32)
    h = (gate * jax.nn.sigmoid(gate) * up).astype(BF16)
    acc_ref[...] += jnp.dot(h, wd_ref[...], preferred_element_type=F32)

    @pl.when(f == pl.num_programs(1) - 1)
    def _():
        y = ALPHA * x_ref[...] + 0.5 * acc_ref[...]
        o_ref[...] = _layer_norm(y, g_ref[...], b_ref[...])


def _ffn_ln(x, wg, wu, wd, g, b):
    t, d = x.shape
    f = wg.shape[1]
    tm, tf = min(FFN_TM, t), FFN_TF
    return pl.pallas_call(
        _ffn_ln_kernel,
        grid=(t // tm, f // tf),
        in_specs=[
            pl.BlockSpec((tm, d), lambda i, j: (i, 0)),
            pl.BlockSpec((d, tf), lambda i, j: (0, j)),
            pl.BlockSpec((d, tf), lambda i, j: (0, j)),
            pl.BlockSpec((tf, d), lambda i, j: (j, 0)),
            pl.BlockSpec((1, d), lambda i, j: (0, 0)),
            pl.BlockSpec((1, d), lambda i, j: (0, 0)),
        ],
        out_specs=pl.BlockSpec((tm, d), lambda i, j: (i, 0)),
        out_shape=jax.ShapeDtypeStruct((t, d), F32),
        scratch_shapes=[pltpu.VMEM((tm, d), BF16), pltpu.VMEM((tm, d), F32)],
        compiler_params=_params("parallel", "arbitrary"),
        name="ffn_ln",
    )(x, wg, wu, wd, g, b)


def _in_proj_kernel(x_ref, wk_ref, wqvut_ref, k_ref, qvut_ref):
    xb = x_ref[...].astype(BF16)
    k_ref[...] = jnp.dot(xb, wk_ref[...], preferred_element_type=F32).astype(BF16)
    qvut = lax.dot_general(wqvut_ref[...], xb, (((1,), (1,)), ((), ())),
                           preferred_element_type=F32)
    qvut_ref[...] = qvut.astype(BF16)


def _in_proj(x, w_ku, w_qv_t):
    t, d = x.shape
    tm = min(PROJ_TM, t)
    n_ku, n_qv = w_ku.shape[1], w_qv_t.shape[0]
    return pl.pallas_call(
        _in_proj_kernel,
        grid=(t // tm,),
        in_specs=[
            pl.BlockSpec((tm, d), lambda i: (i, 0)),
            _resident((d, n_ku)),
            _resident((n_qv, d)),
        ],
        out_specs=[
            pl.BlockSpec((tm, n_ku), lambda i: (i, 0)),
            pl.BlockSpec((n_qv, tm), lambda i: (0, i)),
        ],
        out_shape=[jax.ShapeDtypeStruct((t, n_ku), BF16),
                   jax.ShapeDtypeStruct((n_qv, t), BF16)],
        compiler_params=_params("parallel"),
        name="in_proj",
    )(x, w_ku, w_qv_t)


def _diff_attn_kernel(qt_ref, k_ref, vt_ref, lam_ref, g_ref, o_ref,
                      acc_ref, m_ref, l_ref, ls_ref, smax_ref, redo_ref, *, tq, tk, lam_init):
    i = pl.program_id(2)
    dh = DIFF_HEAD_DIM
    ks, qs = ATTN_KS, ATTN_QS

    qt = qt_ref[...].astype(F32) * (dh ** -0.5 * LOG2_E)
    row = lax.broadcasted_iota(jnp.int32, qt.shape, 0)
    qz = jnp.concatenate([jnp.where(row < dh, qt, 0.0),
                          jnp.where(row >= dh, qt, 0.0)], axis=1).astype(BF16)

    def init():
        m_ref[...] = jnp.full(m_ref.shape, NEG_BIG, F32)
        l_ref[...] = jnp.zeros_like(l_ref)
        acc_ref[...] = jnp.zeros_like(acc_ref)
        ls_ref[...] = jnp.zeros_like(ls_ref)
        smax_ref[...] = jnp.full(smax_ref.shape, NEG_BIG, F32)

    def chunk_mask(shape, k_lo, q_lo):
        kc = (k_lo + lax.broadcasted_iota(jnp.int32, shape, 0)) // CHUNK
        qc = (q_lo + lax.broadcasted_iota(jnp.int32, shape, 1)) // CHUNK
        return kc <= qc

    def exact_step(j, masked):
        k0 = pl.multiple_of(j * tk, tk)
        kb = k_ref[pl.ds(k0, tk), :]
        s = jnp.dot(kb, qz, preferred_element_type=F32)
        if masked:
            keep = chunk_mask((tk, tq), 0, 0)
            s = jnp.where(jnp.concatenate([keep, keep], axis=1), s, NEG_BIG)
        m_old = m_ref[...]
        m_new = jnp.maximum(m_old, jnp.max(s, axis=0, keepdims=True))
        corr = jnp.exp2(m_old - m_new)
        p = jnp.exp2(s - m_new)
        l_ref[...] = corr * l_ref[...] + jnp.sum(p, axis=0, keepdims=True)
        vtb = vt_ref[:, pl.ds(k0, tk)]
        acc_ref[...] = corr * acc_ref[...] + jnp.dot(vtb, p.astype(BF16),
                                                     preferred_element_type=F32)
        m_ref[...] = m_new

    def fast_steps(blocks):
        k0s = [pl.multiple_of(j * tk, tk) for j, _ in blocks]
        tiles = [(c, b, r) for c in range(2 * tq // qs) for b in range(len(blocks))
                 for r in range(tk // ks)
                 if not (blocks[b][1] and (r * ks) // CHUNK > ((c * qs) % tq + qs - 1) // CHUNK)]

        def scores(c, b, r):
            k_lo, q_lo = r * ks, (c * qs) % tq
            kb = k_ref[pl.ds(k0s[b] + k_lo, ks), :]
            s = jnp.dot(kb, qz[:, c * qs:(c + 1) * qs], preferred_element_type=F32)
            if blocks[b][1] and (k_lo + ks - 1) // CHUNK > q_lo // CHUNK:
                s = jnp.where(chunk_mask(s.shape, k_lo, q_lo), s, NEG_BIG)
            return s

        pending = [scores(*t) for t in tiles[:ATTN_LOOKAHEAD]]
        for n, (c, b, r) in enumerate(tiles):
            lanes = slice(c * qs, (c + 1) * qs)
            s = pending.pop(0)
            if n + ATTN_LOOKAHEAD < len(tiles):
                pending.append(scores(*tiles[n + ATTN_LOOKAHEAD]))
            if n == 0 or tiles[n - 1][0] != c:
                shift = m_ref[:, lanes]
                lsum = ls_ref[:, lanes]
                smax = smax_ref[:, lanes]
                pv = None
            smax = jnp.maximum(smax, jnp.max(s.reshape(ks // 8, 8, qs), axis=0))
            e = jnp.exp2(s - shift)
            lsum = lsum + jnp.sum(e.reshape(ks // 8, 8, qs), axis=0)
            vtb = vt_ref[:, pl.ds(k0s[b] + r * ks, ks)]
            d = jnp.dot(vtb, e.astype(BF16), preferred_element_type=F32)
            pv = d if pv is None else pv + d
            if n + 1 == len(tiles) or tiles[n + 1][0] != c:
                acc_ref[:, lanes] += pv
                ls_ref[:, lanes] = lsum
                smax_ref[:, lanes] = smax

    def exact_body(j, carry):
        exact_step(j, False)
        return carry

    def fast_pair_body(p, carry):
        fast_steps([(1 + 2 * p, False), (2 + 2 * p, False)])
        return carry

    redo_ref[0] = 1

    @pl.when(i > 0)
    def _():
        init()
        exact_step(0, False)
        n_full = i - 1
        lax.fori_loop(0, n_full // 2, fast_pair_body, 0)

        @pl.when(n_full % 2 == 1)
        def _():
            fast_steps([(i - 1, False), (i, True)])

        @pl.when(n_full % 2 == 0)
        def _():
            fast_steps([(i, True)])

        excess = jnp.max(smax_ref[...] - m_ref[...])
        redo_ref[0] = (excess > ATTN_SHIFT_SLACK).astype(jnp.int32)

    @pl.when(redo_ref[0] == 1)
    def _():
        init()
        lax.fori_loop(0, i, exact_body, 0)
        exact_step(i, True)

    lam_p = lam_ref[...]
    lam = (jnp.exp(jnp.sum(lam_p[0:1] * lam_p[1:2], axis=-1, keepdims=True))
           - jnp.exp(jnp.sum(lam_p[2:3] * lam_p[3:4], axis=-1, keepdims=True))
           + lam_init)
    acc = acc_ref[...]
    l = l_ref[...] + jnp.sum(ls_ref[...], axis=0, keepdims=True)
    ot = acc[:, :tq] / l[:, :tq] - lam * (acc[:, tq:] / l[:, tq:])
    o = ot.T
    o = o * lax.rsqrt(jnp.mean(o * o, axis=-1, keepdims=True) + RMS_EPS)
    o_ref[...] = (o * g_ref[...] * (1.0 - lam_init)).astype(o_ref.dtype)


def _diff_attn(qvt, ku, lam_p, g, lam_init, batch, seq):
    tq, tk = min(ATTN_TQ, seq), min(ATTN_TK, seq)
    assert tq == tk and tq % CHUNK == 0
    nq = seq // tq
    hh = DIFF_HEADS
    dv = DIFF_V_DIM
    kern = functools.partial(_diff_attn_kernel, tq=tq, tk=tk, lam_init=lam_init)
    return pl.pallas_call(
        kern,
        grid=(batch, hh, nq),
        in_specs=[
            pl.BlockSpec((dv, tq), lambda b, h, i: (h, b * nq + i)),
            pl.BlockSpec((seq, dv), lambda b, h, i: (b, h)),
            pl.BlockSpec((dv, seq), lambda b, h, i: (hh + h, b)),
            pl.BlockSpec((4, DIFF_HEAD_DIM), lambda b, h, i: (0, 0)),
            pl.BlockSpec((1, dv), lambda b, h, i: (0, 0)),
        ],
        out_specs=pl.BlockSpec((tq, dv), lambda b, h, i: (b * nq + i, h)),
        out_shape=jax.ShapeDtypeStruct((batch * seq, hh * dv), BF16),
        scratch_shapes=[pltpu.VMEM((dv, 2 * tq), F32),
                        pltpu.VMEM((1, 2 * tq), F32),
                        pltpu.VMEM((1, 2 * tq), F32),
                        pltpu.VMEM((8, 2 * tq), F32),
                        pltpu.VMEM((8, 2 * tq), F32),
                        pltpu.SMEM((1,), jnp.int32)],
        compiler_params=_params("parallel", "parallel", "arbitrary"),
        name="diff_attn",
    )(qvt, ku, qvt, lam_p, g)


def _s5_kernel(ut_ref, taps_ref, gt_ref, ct_ref, apow_ref, y_ref, mt_ref, *, batch):
    L, P, N = S5_CHUNK, SSM_GROUP, SSM_STATE
    ut = ut_ref[0]
    cols = ut.shape[1]
    per_b = cols // batch

    keep = (lax.broadcasted_iota(jnp.int32, (L, P * L), 1) % L
            <= lax.broadcasted_iota(jnp.int32, (L, P * L), 0))
    for p in range(P):
        base = jnp.broadcast_to(taps_ref[0, p:p + 1, :], (L, P * L))
        rows = pltpu.roll(base, P * L - (L - 1), 1, stride=1, stride_axis=0)
        mt_ref[p * L:(p + 1) * L, :] = jnp.where(keep, rows, 0.0).astype(BF16)

    x = jnp.dot(gt_ref[0], ut, preferred_element_type=F32)
    x_re, x_im = x[:N], x[N:]
    chunk = lax.broadcasted_iota(jnp.int32, (N, cols), 1) % per_b
    for k in range(apow_ref.shape[1]):
        sh = 1 << k
        w_re, w_im = apow_ref[0, k, :, 0:1], apow_ref[0, k, :, 1:2]
        r_re = jnp.where(chunk >= sh, pltpu.roll(x_re, sh, 1), 0.0)
        r_im = jnp.where(chunk >= sh, pltpu.roll(x_im, sh, 1), 0.0)
        x_re, x_im = x_re + (w_re * r_re - w_im * r_im), x_im + (w_re * r_im + w_im * r_re)
    st = jnp.concatenate([jnp.where(chunk >= 1, pltpu.roll(x_re, 1, 1), 0.0),
                          jnp.where(chunk >= 1, pltpu.roll(x_im, 1, 1), 0.0)], axis=0)

    hi = st.astype(BF16)
    lo = (st - hi.astype(F32)).astype(BF16)
    ct = ct_ref[0]
    y = (jnp.dot(mt_ref[...], ut, preferred_element_type=F32)
         + jnp.dot(ct, hi, preferred_element_type=F32)
         + jnp.dot(ct, lo, preferred_element_type=F32))
    y_ref[0] = jax.nn.gelu(y).astype(y_ref.dtype)


def _s5_conv(ut_g, taps, g_t, c_t, a_pow, batch):
    groups, lp, cols = ut_g.shape
    n = SSM_STATE
    kern = functools.partial(_s5_kernel, batch=batch)
    per_group = lambda *shape: pl.BlockSpec((1,) + shape, lambda g: (g,) + (0,) * len(shape))
    return pl.pallas_call(
        kern,
        grid=(groups,),
        in_specs=[per_group(lp, cols), per_group(SSM_GROUP, lp), per_group(2 * n, lp),
                  per_group(lp, 2 * n), per_group(*a_pow.shape[1:])],
        out_specs=per_group(lp, cols),
        out_shape=jax.ShapeDtypeStruct((groups, lp, cols), BF16),
        scratch_shapes=[pltpu.VMEM((lp, lp), BF16)],
        compiler_params=_params("parallel"),
        name="s5_conv",
    )(ut_g, taps, g_t, c_t, a_pow)


def _s5_operators(lam_re, lam_im, log_step, b_re, b_im, c_re, c_im, d_skip, n_chunks):
    hp = lax.Precision.HIGHEST
    L, P, G, N = S5_CHUNK, SSM_GROUP, SSM_GROUPS, SSM_STATE
    lr = jnp.minimum(lam_re.astype(F32), -1e-4)
    li = lam_im.astype(F32)
    step = jnp.exp(log_step.astype(F32))[:, None]
    dr, di = lr * step, li * step
    tau = jnp.arange(L + 1, dtype=F32)[:, None, None]
    mag = jnp.exp(tau * dr)
    pw_re, pw_im = mag * jnp.cos(tau * di), mag * jnp.sin(tau * di)
    a_re, a_im = pw_re[1], pw_im[1]
    den = lr * lr + li * li
    f_re = ((a_re - 1.0) * lr + a_im * li) / den
    f_im = (a_im * lr - (a_re - 1.0) * li) / den
    br, bi = b_re.astype(F32), b_im.astype(F32)
    bb_re = f_re[..., None] * br - f_im[..., None] * bi
    bb_im = f_re[..., None] * bi + f_im[..., None] * br
    cr, ci = c_re.astype(F32), c_im.astype(F32)
    ca_re = cr[None] * pw_re[:, :, None, :] - ci[None] * pw_im[:, :, None, :]
    ca_im = cr[None] * pw_im[:, :, None, :] + ci[None] * pw_re[:, :, None, :]
    taps = (jnp.einsum('tgpn,gnq->gpqt', ca_re[:L], bb_re, precision=hp)
            - jnp.einsum('tgpn,gnq->gpqt', ca_im[:L], bb_im, precision=hp))
    skip = d_skip.astype(F32).reshape(G, P, 1) * jnp.eye(P, dtype=F32)
    taps = taps.at[..., 0].add(skip)
    taps = taps[..., ::-1].reshape(G, P, P * L)
    rev_re = pw_re[L - 1::-1].transpose(1, 2, 0)[:, :, None, :]
    rev_im = pw_im[L - 1::-1].transpose(1, 2, 0)[:, :, None, :]
    g_t = jnp.concatenate([rev_re * bb_re[..., None] - rev_im * bb_im[..., None],
                           rev_re * bb_im[..., None] + rev_im * bb_re[..., None]],
                          axis=1).reshape(G, 2 * N, P * L)
    c_t = jnp.concatenate([ca_re[1:].transpose(1, 2, 0, 3), -ca_im[1:].transpose(1, 2, 0, 3)],
                          axis=-1).reshape(G, P * L, 2 * N)
    n_steps = max(1, (n_chunks - 1).bit_length())
    span = (L * 2.0 ** jnp.arange(n_steps, dtype=F32))[None, :, None]
    mag = jnp.exp(span * dr[:, None, :])
    a_pow = jnp.stack([mag * jnp.cos(span * di[:, None, :]), mag * jnp.sin(span * di[:, None, :])], axis=-1)
    return taps, g_t.astype(BF16), c_t.astype(BF16), a_pow


def _mix_out_kernel(x_ref, attn_ref, y_ref, gw_ref, gb_ref, wo_ref, g_ref, b_ref, o_ref):
    yb = y_ref[...]
    z = jnp.dot(yb, gw_ref[...], preferred_element_type=F32) + gb_ref[...]
    y = yb.astype(F32) * jax.nn.sigmoid(z)
    aw = attn_ref.shape[1]
    mixed = (jnp.dot(attn_ref[...], wo_ref[:aw, :], preferred_element_type=F32)
             + jnp.dot(y.astype(BF16), wo_ref[aw:, :], preferred_element_type=F32))
    o_ref[...] = _layer_norm(ALPHA * x_ref[...] + mixed, g_ref[...], b_ref[...])


def _mix_out(x, attn, y_ssm, glu_w, glu_b, w_out, g, b):
    t, d = x.shape
    tm = min(PROJ_TM, t)
    aw, sw = attn.shape[1], y_ssm.shape[1]
    row = lambda i: (i, 0)
    return pl.pallas_call(
        _mix_out_kernel,
        grid=(t // tm,),
        in_specs=[
            pl.BlockSpec((tm, d), row),
            pl.BlockSpec((tm, aw), row),
            pl.BlockSpec((tm, sw), row),
            _resident((sw, sw)),
            _resident((1, sw)),
            _resident((d, d)),
            _resident((1, d)),
            _resident((1, d)),
        ],
        out_specs=pl.BlockSpec((tm, d), row),
        out_shape=jax.ShapeDtypeStruct((t, d), F32),
        compiler_params=_params("parallel"),
        name="mix_out",
    )(x, attn, y_ssm, glu_w, glu_b, w_out, g, b)


def _mem_proj_kernel(m_ref, wk_ref, wv_ref, k_ref, v_ref):
    mb = m_ref[...].astype(BF16)
    k_ref[...] = jnp.dot(mb, wk_ref[...], preferred_element_type=F32).astype(BF16)
    v_ref[...] = jnp.dot(mb, wv_ref[...], preferred_element_type=F32).astype(BF16)


def _mem_proj(mem2d, wk, wv):
    rows, d = mem2d.shape
    tr = min(256, rows)
    return pl.pallas_call(
        _mem_proj_kernel,
        grid=(rows // tr,),
        in_specs=[pl.BlockSpec((tr, d), lambda i: (i, 0)),
                  _resident((d, d)),
                  _resident((d, d))],
        out_specs=[pl.BlockSpec((tr, d), lambda i: (i, 0)),
                   pl.BlockSpec((tr, d), lambda i: (i, 0))],
        out_shape=[jax.ShapeDtypeStruct((rows, d), BF16)] * 2,
        compiler_params=_params("parallel"),
        name="mem_proj",
    )(mem2d, wk, wv)


def _xattn_ln_kernel(x_ref, wq_ref, k_ref, v_ref, wo_ref, g_ref, b_ref, o_ref, oc_ref):
    x = x_ref[...]
    q = jnp.dot(x.astype(BF16), wq_ref[...], preferred_element_type=F32).astype(BF16)
    hd = XATTN_HEAD_DIM
    for h in range(XATTN_HEADS):
        sl = slice(h * hd, (h + 1) * hd)
        s = lax.dot_general(q[:, sl], k_ref[0, :, sl], (((1,), (1,)), ((), ())),
                            preferred_element_type=F32) * (hd ** -0.5)
        s = s - jnp.max(s, axis=-1, keepdims=True)
        e = jnp.exp(s)
        p = e / jnp.sum(e, axis=-1, keepdims=True)
        oc_ref[:, sl] = jnp.dot(p.astype(BF16), v_ref[0, :, sl],
                                preferred_element_type=F32).astype(BF16)
    out = jnp.dot(oc_ref[...], wo_ref[...], preferred_element_type=F32)
    o_ref[...] = _layer_norm(ALPHA * x + out, g_ref[...], b_ref[...])


def _xattn_ln(x, wq, k_mem, v_mem, wo, g, b, seq):
    t, d = x.shape
    tm = min(PROJ_TM, seq)
    per_b = seq // tm
    n_mem = k_mem.shape[1]
    return pl.pallas_call(
        _xattn_ln_kernel,
        grid=(t // tm,),
        in_specs=[
            pl.BlockSpec((tm, d), lambda i: (i, 0)),
            _resident((d, d)),
            pl.BlockSpec((1, n_mem, d), lambda i: (i // per_b, 0, 0)),
            pl.BlockSpec((1, n_mem, d), lambda i: (i // per_b, 0, 0)),
            _resident((d, d)),
            _resident((1, d)),
            _resident((1, d)),
        ],
        out_specs=pl.BlockSpec((tm, d), lambda i: (i, 0)),
        out_shape=jax.ShapeDtypeStruct((t, d), F32),
        scratch_shapes=[pltpu.VMEM((tm, d), BF16)],
        compiler_params=_params("parallel"),
        name="xattn_ln",
    )(x, wq, k_mem, v_mem, wo, g, b)


def kernel(x, mem, ffn1_w_gate, ffn1_w_up, ffn1_w_down, ln1_g, ln1_b, w_in, lambda_q1, lambda_k1, lambda_q2, lambda_k2, diff_norm_g, ssm_lambda_re, ssm_lambda_im, ssm_log_step, ssm_b_re, ssm_b_im, ssm_c_re, ssm_c_im, ssm_d, ssm_glu_w, ssm_glu_b, w_out, ln2_g, ln2_b, xattn_w_q, xattn_w_k, xattn_w_v, xattn_w_o, ln3_g, ln3_b, ffn2_w_gate, ffn2_w_up, ffn2_w_down, ln4_g, ln4_b):
    batch, seq, d = x.shape
    t = batch * seq
    n_mem = mem.shape[1]
    aw = ATTN_WIDTH
    L, P, G = S5_CHUNK, SSM_GROUP, SSM_GROUPS
    n_chunks = seq // L
    xf = x.reshape(t, d)
    mem2d = mem.reshape(batch * n_mem, d)
    row = lambda a: a.reshape(1, -1).astype(F32)

    for l in range(DEPTH):
        lam_init = 0.8 - 0.6 * math.exp(-0.3 * l)
        xf = _ffn_ln(xf, ffn1_w_gate[l].astype(BF16), ffn1_w_up[l].astype(BF16),
                     ffn1_w_down[l].astype(BF16), row(ln1_g[l]), row(ln1_b[l]))

        w = w_in[l]
        w_k = w[:, aw:2 * aw].astype(BF16)
        w_qvu_t = jnp.concatenate([w[:, :aw], w[:, 2 * aw:]], axis=1).T.astype(BF16)
        k_nat, qvut = _in_proj(xf, w_k, w_qvu_t)

        lam_p = jnp.stack([lambda_q1[l], lambda_k1[l], lambda_q2[l], lambda_k2[l]]).astype(F32)
        attn = _diff_attn(qvut, k_nat, lam_p, row(diff_norm_g[l]), lam_init, batch, seq)

        ops = _s5_operators(ssm_lambda_re[l], ssm_lambda_im[l], ssm_log_step[l],
                            ssm_b_re[l], ssm_b_im[l], ssm_c_re[l], ssm_c_im[l], ssm_d[l], n_chunks)
        ut_g = (qvut[2 * aw:].reshape(G * P, batch * n_chunks, L)
                .swapaxes(1, 2).reshape(G, P * L, batch * n_chunks))
        yt_g = _s5_conv(ut_g, *ops, batch)
        y_ssm = yt_g.reshape(G * P, L, batch * n_chunks).transpose(2, 1, 0).reshape(t, G * P)

        xf = _mix_out(xf, attn, y_ssm, ssm_glu_w[l].astype(BF16),
                      row(ssm_glu_b[l]), w_out[l].astype(BF16), row(ln2_g[l]), row(ln2_b[l]))

        k_mem, v_mem = _mem_proj(mem2d, xattn_w_k[l].astype(BF16), xattn_w_v[l].astype(BF16))
        xf = _xattn_ln(xf, xattn_w_q[l].astype(BF16), k_mem.reshape(batch, n_mem, d),
                       v_mem.reshape(batch, n_mem, d), xattn_w_o[l].astype(BF16),
                       row(ln3_g[l]), row(ln3_b[l]), seq)

        xf = _ffn_ln(xf, ffn2_w_gate[l].astype(BF16), ffn2_w_up[l].astype(BF16),
                     ffn2_w_down[l].astype(BF16), row(ln4_g[l]), row(ln4_b[l]))
    return xf.reshape(batch, seq, d)
```

```python
import functools
import math

import jax
import jax.numpy as jnp
from jax import lax
from jax.experimental import pallas as pl
from jax.experimental.pallas import tpu as pltpu

F32 = jnp.float32
BF16 = jnp.bfloat16

D_MODEL = 2048
DEPTH = 4
CHUNK = 64
ATTN_WIDTH = D_MODEL // 2
SSM_WIDTH = D_MODEL - ATTN_WIDTH
DIFF_HEAD_DIM = 64
DIFF_V_DIM = 2 * DIFF_HEAD_DIM
DIFF_HEADS = ATTN_WIDTH // DIFF_V_DIM
SSM_GROUP = 16
SSM_GROUPS = SSM_WIDTH // SSM_GROUP
SSM_STATE = 64
XATTN_HEADS = 4
XATTN_HEAD_DIM = D_MODEL // XATTN_HEADS
ALPHA = (2 * DEPTH) ** 0.25
LN_EPS = 1e-5
RMS_EPS = 1e-5
NEG_BIG = -1e30
LOG2_E = 1.4426950408889634

V7X_VMEM_LIMIT_BYTES = 56 * 1024 * 1024

FFN_TM = 512
FFN_TF = 512
PROJ_TM = 512
ATTN_TQ = 1024
ATTN_TK = 512
ATTN_KS = 256
ATTN_QS = 512
ATTN_LOOKAHEAD = 2
ATTN_SHIFT_SLACK = 64.0
S5_CHUNK = 64


def _params(*sem):
    return pltpu.CompilerParams(dimension_semantics=sem,
                                vmem_limit_bytes=V7X_VMEM_LIMIT_BYTES)


def _resident(shape):
    return pl.BlockSpec(shape, lambda *_: (0,) * len(shape), pipeline_mode=pl.Buffered(1))


def _layer_norm(y, g, b):
    mu = jnp.mean(y, axis=-1, keepdims=True)
    yc = y - mu
    var = jnp.mean(yc * yc, axis=-1, keepdims=True)
    return yc * lax.rsqrt(var + LN_EPS) * g + b


def _ffn_ln_kernel(x_ref, wg_ref, wu_ref, wd_ref, g_ref, b_ref, o_ref, xb_ref, acc_ref):
    f = pl.program_id(1)

    @pl.when(f == 0)
    def _():
        xb_ref[...] = x_ref[...].astype(BF16)
        acc_ref[...] = jnp.zeros_like(acc_ref)

    xb = xb_ref[...]
    gate = jnp.dot(xb, wg_ref[...], preferred_element_type=F32)
    up = jnp.dot(xb, wu_ref[...], preferred_element_type=F32)
    h = (gate * jax.nn.sigmoid(gate) * up).astype(BF16)
    acc_ref[...] += jnp.dot(h, wd_ref[...], preferred_element_type=F32)

    @pl.when(f == pl.num_programs(1) - 1)
    def _():
        y = ALPHA * x_ref[...] + 0.5 * acc_ref[...]
        o_ref[...] = _layer_norm(y, g_ref[...], b_ref[...])


def _ffn_ln(x, wg, wu, wd, g, b):
    t, d = x.shape
    f = wg.shape[1]
    tm, tf = min(FFN_TM, t), FFN_TF
    return pl.pallas_call(
        _ffn_ln_kernel,
        grid=(t // tm, f // tf),
        in_specs=[
            pl.BlockSpec((tm, d), lambda i, j: (i, 0)),
            pl.BlockSpec((d, tf), lambda i, j: (0, j)),
            pl.BlockSpec((d, tf), lambda i, j: (0, j)),
            pl.BlockSpec((tf, d), lambda i, j: (j, 0)),
            pl.BlockSpec((1, d), lambda i, j: (0, 0)),
            pl.BlockSpec((1, d), lambda i, j: (0, 0)),
        ],
        out_specs=pl.BlockSpec((tm, d), lambda i, j: (i, 0)),
        out_shape=jax.ShapeDtypeStruct((t, d), F32),
        scratch_shapes=[pltpu.VMEM((tm, d), BF16), pltpu.VMEM((tm, d), F32)],
        compiler_params=_params("parallel", "arbitrary"),
        name="ffn_ln",
    )(x, wg, wu, wd, g, b)


def _in_proj_kernel(x_ref, wk_ref, wqvut_ref, k_ref, qvut_ref):
    xb = x_ref[...].astype(BF16)
    k_ref[...] = jnp.dot(xb, wk_ref[...], preferred_element_type=F32).astype(BF16)
    qvut = lax.dot_general(wqvut_ref[...], xb, (((1,), (1,)), ((), ())),
                           preferred_element_type=F32)
    qvut_ref[...] = qvut.astype(BF16)


def _in_proj(x, w_ku, w_qv_t):
    t, d = x.shape
    tm = min(PROJ_TM, t)
    n_ku, n_qv = w_ku.shape[1], w_qv_t.shape[0]
    return pl.pallas_call(
        _in_proj_kernel,
        grid=(t // tm,),
        in_specs=[
            pl.BlockSpec((tm, d), lambda i: (i, 0)),
            _resident((d, n_ku)),
            _resident((n_qv, d)),
        ],
        out_specs=[
            pl.BlockSpec((tm, n_ku), lambda i: (i, 0)),
            pl.BlockSpec((n_qv, tm), lambda i: (0, i)),
        ],
        out_shape=[jax.ShapeDtypeStruct((t, n_ku), BF16),
                   jax.ShapeDtypeStruct((n_qv, t), BF16)],
        compiler_params=_params("parallel"),
        name="in_proj",
    )(x, w_ku, w_qv_t)


def _diff_attn_kernel(qt_ref, k_ref, vt_ref, lam_ref, g_ref, o_ref,
                      acc_ref, m_ref, l_ref, ls_ref, smax_ref, *, tq, tk, lam_init):
    i = pl.program_id(2)
    dh = DIFF_HEAD_DIM
    ks, qs = ATTN_KS, ATTN_QS
    nkb = tq // tk
    n_strips, n_sub = 2 * tq // qs, tk // ks

    qt = qt_ref[...].astype(F32) * (dh ** -0.5 * LOG2_E)
    row = lax.broadcasted_iota(jnp.int32, qt.shape, 0)
    qz = jnp.concatenate([jnp.where(row < dh, qt, 0.0),
                          jnp.where(row >= dh, qt, 0.0)], axis=1).astype(BF16)

    def init():
        m_ref[...] = jnp.full(m_ref.shape, NEG_BIG, F32)
        l_ref[...] = jnp.zeros_like(l_ref)
        acc_ref[...] = jnp.zeros_like(acc_ref)
        ls_ref[...] = jnp.zeros_like(ls_ref)
        smax_ref[...] = jnp.full(smax_ref.shape, NEG_BIG, F32)

    def chunk_mask(shape, k_lo, q_lo):
        kc = (k_lo + lax.broadcasted_iota(jnp.int32, shape, 0)) // CHUNK
        qc = (q_lo + lax.broadcasted_iota(jnp.int32, shape, 1)) // CHUNK
        return kc <= qc

    def exact_step(j, koff):
        k0 = pl.multiple_of(j * tk, tk)
        kb = k_ref[pl.ds(k0, tk), :]
        s = jnp.dot(kb, qz, preferred_element_type=F32)
        if koff is not None:
            keep = chunk_mask((tk, tq), koff, 0)
            s = jnp.where(jnp.concatenate([keep, keep], axis=1), s, NEG_BIG)
        m_old = m_ref[...]
        m_new = jnp.maximum(m_old, jnp.max(s, axis=0, keepdims=True))
        corr = jnp.exp2(m_old - m_new)
        p = jnp.exp2(s - m_new)
        l_ref[...] = corr * l_ref[...] + jnp.sum(p, axis=0, keepdims=True)
        vtb = vt_ref[:, pl.ds(k0, tk)]
        acc_ref[...] = corr * acc_ref[...] + jnp.dot(vtb, p.astype(BF16),
                                                     preferred_element_type=F32)
        m_ref[...] = m_new

    def scores(j, koff, c, r):
        q_lo = (c * qs) % tq
        kb = k_ref[pl.ds(pl.multiple_of(j * tk, tk) + r * ks, ks), :]
        s = jnp.dot(kb, qz[:, c * qs:(c + 1) * qs], preferred_element_type=F32)
        if koff is not None and (koff + r * ks + ks - 1) // CHUNK > q_lo // CHUNK:
            s = jnp.where(chunk_mask(s.shape, koff + r * ks, q_lo), s, NEG_BIG)
        return s

    def prologue(koff):
        init()
        for c in range(n_strips):
            m_ref[:, c * qs:(c + 1) * qs] = jnp.max(scores(0, koff, c, 0), axis=0, keepdims=True)

    def fast_steps(blocks):
        def above_diagonal(c, b, r):
            koff = blocks[b][1]
            return koff is not None and (koff + r * ks) // CHUNK > ((c * qs) % tq + qs - 1) // CHUNK

        tiles = [(c, b, r) for c in range(n_strips) for b in range(len(blocks)) for r in range(n_sub)
                 if not above_diagonal(c, b, r)]
        score = lambda c, b, r: scores(blocks[b][0], blocks[b][1], c, r)
        in_flight = [score(*t) for t in tiles[:ATTN_LOOKAHEAD]]
        for n, (c, b, r) in enumerate(tiles):
            lanes = slice(c * qs, (c + 1) * qs)
            s = in_flight.pop(0)
            if n + ATTN_LOOKAHEAD < len(tiles):
                in_flight.append(score(*tiles[n + ATTN_LOOKAHEAD]))
            if n == 0 or tiles[n - 1][0] != c:
                shift = m_ref[:, lanes]
                lsum = ls_ref[:, lanes]
                smax = smax_ref[:, lanes]
                pv = None
            smax = jnp.maximum(smax, jnp.max(s.reshape(ks // 8, 8, qs), axis=0))
            e = jnp.exp2(s - shift)
            lsum = lsum + jnp.sum(e.reshape(ks // 8, 8, qs), axis=0)
            vtb = vt_ref[:, pl.ds(pl.multiple_of(blocks[b][0] * tk, tk) + r * ks, ks)]
            d = jnp.dot(vtb, e.astype(BF16), preferred_element_type=F32)
            pv = d if pv is None else pv + d
            if n + 1 == len(tiles) or tiles[n + 1][0] != c:
                acc_ref[:, lanes] += pv
                ls_ref[:, lanes] = lsum
                smax_ref[:, lanes] = smax

    diagonal = [(nkb * i + d, d * tk) for d in range(nkb)]

    def exact_body(j, carry):
        exact_step(j, None)
        return carry

    def fast_body(p, carry):
        fast_steps([(nkb * p + d, None) for d in range(nkb)])
        return carry

    @pl.when(i == 0)
    def _():
        prologue(0)

    @pl.when(i > 0)
    def _():
        prologue(None)
        lax.fori_loop(0, i, fast_body, 0)

    fast_steps(diagonal)
    excess = jnp.max(smax_ref[...] - m_ref[...])

    @pl.when(excess > ATTN_SHIFT_SLACK)
    def _():
        init()
        lax.fori_loop(0, nkb * i, exact_body, 0)
        for j, koff in diagonal:
            exact_step(j, koff)

    lam_p = lam_ref[...]
    lam = (jnp.exp(jnp.sum(lam_p[0:1] * lam_p[1:2], axis=-1, keepdims=True))
           - jnp.exp(jnp.sum(lam_p[2:3] * lam_p[3:4], axis=-1, keepdims=True))
           + lam_init)
    acc = acc_ref[...]
    l = l_ref[...] + jnp.sum(ls_ref[...], axis=0, keepdims=True)
    ot = acc[:, :tq] / l[:, :tq] - lam * (acc[:, tq:] / l[:, tq:])
    o = ot.T
    o = o * lax.rsqrt(jnp.mean(o * o, axis=-1, keepdims=True) + RMS_EPS)
    o_ref[...] = (o * g_ref[...] * (1.0 - lam_init)).astype(o_ref.dtype)


def _diff_attn(qvt, ku, lam_p, g, lam_init, batch, seq):
    tk = min(ATTN_TK, seq)
    tq = min(ATTN_TQ, seq)
    assert tq % tk == 0 and tk % CHUNK == 0 and seq % tq == 0
    nq = seq // tq
    hh = DIFF_HEADS
    dv = DIFF_V_DIM
    kern = functools.partial(_diff_attn_kernel, tq=tq, tk=tk, lam_init=lam_init)
    return pl.pallas_call(
        kern,
        grid=(batch, hh, nq),
        in_specs=[
            pl.BlockSpec((dv, tq), lambda b, h, i: (h, b * nq + i)),
            pl.BlockSpec((seq, dv), lambda b, h, i: (b, h)),
            pl.BlockSpec((dv, seq), lambda b, h, i: (hh + h, b)),
            pl.BlockSpec((4, DIFF_HEAD_DIM), lambda b, h, i: (0, 0)),
            pl.BlockSpec((1, dv), lambda b, h, i: (0, 0)),
        ],
        out_specs=pl.BlockSpec((tq, dv), lambda b, h, i: (b * nq + i, h)),
        out_shape=jax.ShapeDtypeStruct((batch * seq, hh * dv), BF16),
        scratch_shapes=[pltpu.VMEM((dv, 2 * tq), F32),
                        pltpu.VMEM((1, 2 * tq), F32),
                        pltpu.VMEM((1, 2 * tq), F32),
                        pltpu.VMEM((8, 2 * tq), F32),
                        pltpu.VMEM((8, 2 * tq), F32)],
        compiler_params=_params("parallel", "parallel", "arbitrary"),
        name="diff_attn",
    )(qvt, ku, qvt, lam_p, g)


def _s5_kernel(ut_ref, taps_ref, gt_ref, ct_ref, apow_ref, y_ref, mt_ref, *, batch):
    L, P, N = S5_CHUNK, SSM_GROUP, SSM_STATE
    ut = ut_ref[0]
    cols = ut.shape[1]
    per_b = cols // batch

    keep = (lax.broadcasted_iota(jnp.int32, (L, P * L), 1) % L
            <= lax.broadcasted_iota(jnp.int32, (L, P * L), 0))
    for p in range(P):
        base = jnp.broadcast_to(taps_ref[0, p:p + 1, :], (L, P * L))
        rows = pltpu.roll(base, P * L - (L - 1), 1, stride=1, stride_axis=0)
        mt_ref[p * L:(p + 1) * L, :] = jnp.where(keep, rows, 0.0).astype(BF16)

    x = jnp.dot(gt_ref[0], ut, preferred_element_type=F32)
    x_re, x_im = x[:N], x[N:]
    chunk = lax.broadcasted_iota(jnp.int32, (N, cols), 1) % per_b
    for k in range(apow_ref.shape[1]):
        sh = 1 << k
        w_re, w_im = apow_ref[0, k, :, 0:1], apow_ref[0, k, :, 1:2]
        r_re = jnp.where(chunk >= sh, pltpu.roll(x_re, sh, 1), 0.0)
        r_im = jnp.where(chunk >= sh, pltpu.roll(x_im, sh, 1), 0.0)
        x_re, x_im = x_re + (w_re * r_re - w_im * r_im), x_im + (w_re * r_im + w_im * r_re)
    st = jnp.concatenate([jnp.where(chunk >= 1, pltpu.roll(x_re, 1, 1), 0.0),
                          jnp.where(chunk >= 1, pltpu.roll(x_im, 1, 1), 0.0)], axis=0)

    hi = st.astype(BF16)
    lo = (st - hi.astype(F32)).astype(BF16)
    ct = ct_ref[0]
    y = (jnp.dot(mt_ref[...], ut, preferred_element_type=F32)
         + jnp.dot(ct, hi, preferred_element_type=F32)
         + jnp.dot(ct, lo, preferred_element_type=F32))
    y_ref[0] = jax.nn.gelu(y).astype(y_ref.dtype)


def _s5_conv(ut_g, taps, g_t, c_t, a_pow, batch):
    groups, lp, cols = ut_g.shape
    n = SSM_STATE
    kern = functools.partial(_s5_kernel, batch=batch)
    per_group = lambda *shape: pl.BlockSpec((1,) + shape, lambda g: (g,) + (0,) * len(shape))
    return pl.pallas_call(
        kern,
        grid=(groups,),
        in_specs=[per_group(lp, cols), per_group(SSM_GROUP, lp), per_group(2 * n, lp),
                  per_group(lp, 2 * n), per_group(*a_pow.shape[1:])],
        out_specs=per_group(lp, cols),
        out_shape=jax.ShapeDtypeStruct((groups, lp, cols), BF16),
        scratch_shapes=[pltpu.VMEM((lp, lp), BF16)],
        compiler_params=_params("parallel"),
        name="s5_conv",
    )(ut_g, taps, g_t, c_t, a_pow)


def _s5_operators(lam_re, lam_im, log_step, b_re, b_im, c_re, c_im, d_skip, n_chunks):
    hp = lax.Precision.HIGHEST
    L, P, G, N = S5_CHUNK, SSM_GROUP, SSM_GROUPS, SSM_STATE
    lr = jnp.minimum(lam_re.astype(F32), -1e-4)
    li = lam_im.astype(F32)
    step = jnp.exp(log_step.astype(F32))[:, None]
    dr, di = lr * step, li * step
    tau = jnp.arange(L + 1, dtype=F32)[:, None, None]
    mag = jnp.exp(tau * dr)
    pw_re, pw_im = mag * jnp.cos(tau * di), mag * jnp.sin(tau * di)
    a_re, a_im = pw_re[1], pw_im[1]
    den = lr * lr + li * li
    f_re = ((a_re - 1.0) * lr + a_im * li) / den
    f_im = (a_im * lr - (a_re - 1.0) * li) / den
    br, bi = b_re.astype(F32), b_im.astype(F32)
    bb_re = f_re[..., None] * br - f_im[..., None] * bi
    bb_im = f_re[..., None] * bi + f_im[..., None] * br
    cr, ci = c_re.astype(F32), c_im.astype(F32)
    ca_re = cr[None] * pw_re[:, :, None, :] - ci[None] * pw_im[:, :, None, :]
    ca_im = cr[None] * pw_im[:, :, None, :] + ci[None] * pw_re[:, :, None, :]
    taps = (jnp.einsum('tgpn,gnq->gpqt', ca_re[:L], bb_re, precision=hp)
            - jnp.einsum('tgpn,gnq->gpqt', ca_im[:L], bb_im, precision=hp))
    skip = d_skip.astype(F32).reshape(G, P, 1) * jnp.eye(P, dtype=F32)
    taps = taps.at[..., 0].add(skip)
    taps = taps[..., ::-1].reshape(G, P, P * L)
    rev_re = pw_re[L - 1::-1].transpose(1, 2, 0)[:, :, None, :]
    rev_im = pw_im[L - 1::-1].transpose(1, 2, 0)[:, :, None, :]
    g_t = jnp.concatenate([rev_re * bb_re[..., None] - rev_im * bb_im[..., None],
                           rev_re * bb_im[..., None] + rev_im * bb_re[..., None]],
                          axis=1).reshape(G, 2 * N, P * L)
    c_t = jnp.concatenate([ca_re[1:].transpose(1, 2, 0, 3), -ca_im[1:].transpose(1, 2, 0, 3)],
                          axis=-1).reshape(G, P * L, 2 * N)
    n_steps = max(1, (n_chunks - 1).bit_length())
    span = (L * 2.0 ** jnp.arange(n_steps, dtype=F32))[None, :, None]
    mag = jnp.exp(span * dr[:, None, :])
    a_pow = jnp.stack([mag * jnp.cos(span * di[:, None, :]), mag * jnp.sin(span * di[:, None, :])], axis=-1)
    return taps, g_t.astype(BF16), c_t.astype(BF16), a_pow


def _mix_out_kernel(x_ref, attn_ref, y_ref, gw_ref, gb_ref, wo_ref, g_ref, b_ref, o_ref):
    yb = y_ref[...]
    z = jnp.dot(yb, gw_ref[...], preferred_element_type=F32) + gb_ref[...]
    y = yb.astype(F32) * jax.nn.sigmoid(z)
    aw = attn_ref.shape[1]
    mixed = (jnp.dot(attn_ref[...], wo_ref[:aw, :], preferred_element_type=F32)
             + jnp.dot(y.astype(BF16), wo_ref[aw:, :], preferred_element_type=F32))
    o_ref[...] = _layer_norm(ALPHA * x_ref[...] + mixed, g_ref[...], b_ref[...])


def _mix_out(x, attn, y_ssm, glu_w, glu_b, w_out, g, b):
    t, d = x.shape
    tm = min(PROJ_TM, t)
    aw, sw = attn.shape[1], y_ssm.shape[1]
    row = lambda i: (i, 0)
    return pl.pallas_call(
        _mix_out_kernel,
        grid=(t // tm,),
        in_specs=[
            pl.BlockSpec((tm, d), row),
            pl.BlockSpec((tm, aw), row),
            pl.BlockSpec((tm, sw), row),
            _resident((sw, sw)),
            _resident((1, sw)),
            _resident((d, d)),
            _resident((1, d)),
            _resident((1, d)),
        ],
        out_specs=pl.BlockSpec((tm, d), row),
        out_shape=jax.ShapeDtypeStruct((t, d), F32),
        compiler_params=_params("parallel"),
        name="mix_out",
    )(x, attn, y_ssm, glu_w, glu_b, w_out, g, b)


def _mem_fold_kernel(m_ref, wq_ref, wk_ref, wv_ref, wo_ref, qk_ref, vo_ref):
    mb = m_ref[0].astype(BF16)
    k = jnp.dot(mb, wk_ref[...], preferred_element_type=F32).astype(BF16)
    v = jnp.dot(mb, wv_ref[...], preferred_element_type=F32).astype(BF16)
    qk = lax.dot_general(wq_ref[...], k, (((1,), (1,)), ((), ())),
                         preferred_element_type=F32)
    qk_ref[0] = (qk * (XATTN_HEAD_DIM ** -0.5)).astype(BF16)
    vo_ref[0] = jnp.dot(v, wo_ref[...], preferred_element_type=F32).astype(BF16)


def _mem_fold(mem, wq, wk, wv, wo):
    batch, n_mem, d = mem.shape
    hd = XATTN_HEAD_DIM
    col = lambda b, h: (0, h)
    return pl.pallas_call(
        _mem_fold_kernel,
        grid=(batch, XATTN_HEADS),
        in_specs=[pl.BlockSpec((1, n_mem, d), lambda b, h: (b, 0, 0)),
                  pl.BlockSpec((d, hd), col),
                  pl.BlockSpec((d, hd), col),
                  pl.BlockSpec((d, hd), col),
                  pl.BlockSpec((hd, d), lambda b, h: (h, 0))],
        out_specs=[pl.BlockSpec((1, d, n_mem), lambda b, h: (b, 0, h)),
                   pl.BlockSpec((1, n_mem, d), lambda b, h: (b, h, 0))],
        out_shape=[jax.ShapeDtypeStruct((batch, d, XATTN_HEADS * n_mem), BF16),
                   jax.ShapeDtypeStruct((batch, XATTN_HEADS * n_mem, d), BF16)],
        compiler_params=_params("parallel", "parallel"),
        name="mem_fold",
    )(mem, wq, wk, wv, wo)


def _xattn_ln_kernel(x_ref, qk_ref, vo_ref, g_ref, b_ref, o_ref, *, n_mem):
    x = x_ref[...]
    s = jnp.dot(x.astype(BF16), qk_ref[0], preferred_element_type=F32)
    probs = []
    for h in range(XATTN_HEADS):
        sh = s[:, h * n_mem:(h + 1) * n_mem]
        e = jnp.exp(sh - jnp.max(sh, axis=-1, keepdims=True))
        probs.append((e / jnp.sum(e, axis=-1, keepdims=True)).astype(BF16))
    out = jnp.dot(jnp.concatenate(probs, axis=1), vo_ref[0], preferred_element_type=F32)
    o_ref[...] = _layer_norm(ALPHA * x + out, g_ref[...], b_ref[...])


def _xattn_ln(x, qk, vo, g, b, seq):
    t, d = x.shape
    tm = min(PROJ_TM, seq)
    per_b = seq // tm
    hm = qk.shape[2]
    kern = functools.partial(_xattn_ln_kernel, n_mem=hm // XATTN_HEADS)
    return pl.pallas_call(
        kern,
        grid=(t // tm,),
        in_specs=[
            pl.BlockSpec((tm, d), lambda i: (i, 0)),
            pl.BlockSpec((1, d, hm), lambda i: (i // per_b, 0, 0)),
            pl.BlockSpec((1, hm, d), lambda i: (i // per_b, 0, 0)),
            _resident((1, d)),
            _resident((1, d)),
        ],
        out_specs=pl.BlockSpec((tm, d), lambda i: (i, 0)),
        out_shape=jax.ShapeDtypeStruct((t, d), F32),
        compiler_params=_params("parallel"),
        name="xattn_ln",
    )(x, qk, vo, g, b)


def kernel(x, mem, ffn1_w_gate, ffn1_w_up, ffn1_w_down, ln1_g, ln1_b, w_in, lambda_q1, lambda_k1, lambda_q2, lambda_k2, diff_norm_g, ssm_lambda_re, ssm_lambda_im, ssm_log_step, ssm_b_re, ssm_b_im, ssm_c_re, ssm_c_im, ssm_d, ssm_glu_w, ssm_glu_b, w_out, ln2_g, ln2_b, xattn_w_q, xattn_w_k, xattn_w_v, xattn_w_o, ln3_g, ln3_b, ffn2_w_gate, ffn2_w_up, ffn2_w_down, ln4_g, ln4_b):
    batch, seq, d = x.shape
    t = batch * seq
    aw = ATTN_WIDTH
    L, P, G = S5_CHUNK, SSM_GROUP, SSM_GROUPS
    n_chunks = seq // L
    xf = x.reshape(t, d)
    row = lambda a: a.reshape(1, -1).astype(F32)

    for l in range(DEPTH):
        lam_init = 0.8 - 0.6 * math.exp(-0.3 * l)
        xf = _ffn_ln(xf, ffn1_w_gate[l].astype(BF16), ffn1_w_up[l].astype(BF16),
                     ffn1_w_down[l].astype(BF16), row(ln1_g[l]), row(ln1_b[l]))

        w = w_in[l]
        w_k = w[:, aw:2 * aw].astype(BF16)
        w_qvu_t = jnp.concatenate([w[:, :aw], w[:, 2 * aw:]], axis=1).T.astype(BF16)
        k_nat, qvut = _in_proj(xf, w_k, w_qvu_t)

        lam_p = jnp.stack([lambda_q1[l], lambda_k1[l], lambda_q2[l], lambda_k2[l]]).astype(F32)
        attn = _diff_attn(qvut, k_nat, lam_p, row(diff_norm_g[l]), lam_init, batch, seq)

        ops = _s5_operators(ssm_lambda_re[l], ssm_lambda_im[l], ssm_log_step[l],
                            ssm_b_re[l], ssm_b_im[l], ssm_c_re[l], ssm_c_im[l], ssm_d[l], n_chunks)
        ut_g = (qvut[2 * aw:].reshape(G * P, batch * n_chunks, L)
                .swapaxes(1, 2).reshape(G, P * L, batch * n_chunks))
        yt_g = _s5_conv(ut_g, *ops, batch)
        y_ssm = yt_g.reshape(G * P, L, batch * n_chunks).transpose(2, 1, 0).reshape(t, G * P)

        xf = _mix_out(xf, attn, y_ssm, ssm_glu_w[l].astype(BF16),
                      row(ssm_glu_b[l]), w_out[l].astype(BF16), row(ln2_g[l]), row(ln2_b[l]))

        qk, vo = _mem_fold(mem, xattn_w_q[l].astype(BF16), xattn_w_k[l].astype(BF16),
                           xattn_w_v[l].astype(BF16), xattn_w_o[l].astype(BF16))
        xf = _xattn_ln(xf, qk, vo, row(ln3_g[l]), row(ln3_b[l]), seq)

        xf = _ffn_ln(xf, ffn2_w_gate[l].astype(BF16), ffn2_w_up[l].astype(BF16),
                     ffn2_w_down[l].astype(BF16), row(ln4_g[l]), row(ln4_b[l]))
    return xf.reshape(batch, seq, d)
```

```python
import functools
import math

import jax
import jax.numpy as jnp
from jax import lax
from jax.experimental import pallas as pl
from jax.experimental.pallas import tpu as pltpu

F32 = jnp.float32
BF16 = jnp.bfloat16

D_MODEL = 2048
DEPTH = 4
CHUNK = 64
ATTN_WIDTH = D_MODEL // 2
SSM_WIDTH = D_MODEL - ATTN_WIDTH
DIFF_HEAD_DIM = 64
DIFF_V_DIM = 2 * DIFF_HEAD_DIM
DIFF_HEADS = ATTN_WIDTH // DIFF_V_DIM
SSM_GROUP = 16
SSM_GROUPS = SSM_WIDTH // SSM_GROUP
SSM_STATE = 64
XATTN_HEADS = 4
XATTN_HEAD_DIM = D_MODEL // XATTN_HEADS
ALPHA = (2 * DEPTH) ** 0.25
LN_EPS = 1e-5
RMS_EPS = 1e-5
NEG_BIG = -1e30
LOG2_E = 1.4426950408889634

V7X_VMEM_LIMIT_BYTES = 56 * 1024 * 1024

FFN_TM = 512
FFN_TF = 512
PROJ_TM = 512
ATTN_TQ = 1024
ATTN_TK = 512
ATTN_KS = 256
ATTN_QS = 512
ATTN_LOOKAHEAD = 2
ATTN_SHIFT_SLACK = 64.0
S5_CHUNK = 64


def _params(*sem):
    return pltpu.CompilerParams(dimension_semantics=sem,
                                vmem_limit_bytes=V7X_VMEM_LIMIT_BYTES)


def _resident(shape):
    return pl.BlockSpec(shape, lambda *_: (0,) * len(shape), pipeline_mode=pl.Buffered(1))


def _layer_norm(y, g, b):
    mu = jnp.mean(y, axis=-1, keepdims=True)
    yc = y - mu
    var = jnp.mean(yc * yc, axis=-1, keepdims=True)
    return yc * lax.rsqrt(var + LN_EPS) * g + b


def _ffn_ln_kernel(x_ref, wg_ref, wu_ref, wd_ref, g_ref, b_ref, o_ref, xb_ref, acc_ref):
    f = pl.program_id(1)

    @pl.when(f == 0)
    def _():
        xb_ref[...] = x_ref[...].astype(BF16)
        acc_ref[...] = jnp.zeros_like(acc_ref)

    xb = xb_ref[...]
    gate = jnp.dot(xb, wg_ref[...], preferred_element_type=F32)
    up = jnp.dot(xb, wu_ref[...], preferred_element_type=F32)
    h = (gate * jax.nn.sigmoid(gate) * up).astype(BF16)
    acc_ref[...] += jnp.dot(h, wd_ref[...], preferred_element_type=F32)

    @pl.when(f == pl.num_programs(1) - 1)
    def _():
        y = ALPHA * x_ref[...] + 0.5 * acc_ref[...]
        o_ref[...] = _layer_norm(y, g_ref[...], b_ref[...])


def _ffn_ln(x, wg, wu, wd, g, b):
    t, d = x.shape
    f = wg.shape[1]
    tm, tf = min(FFN_TM, t), FFN_TF
    return pl.pallas_call(
        _ffn_ln_kernel,
        grid=(t // tm, f // tf),
        in_specs=[
            pl.BlockSpec((tm, d), lambda i, j: (i, 0)),
            pl.BlockSpec((d, tf), lambda i, j: (0, j)),
            pl.BlockSpec((d, tf), lambda i, j: (0, j)),
            pl.BlockSpec((tf, d), lambda i, j: (j, 0)),
            pl.BlockSpec((1, d), lambda i, j: (0, 0)),
            pl.BlockSpec((1, d), lambda i, j: (0, 0)),
        ],
        out_specs=pl.BlockSpec((tm, d), lambda i, j: (i, 0)),
        out_shape=jax.ShapeDtypeStruct((t, d), F32),
        scratch_shapes=[pltpu.VMEM((tm, d), BF16), pltpu.VMEM((tm, d), F32)],
        compiler_params=_params("parallel", "arbitrary"),
        name="ffn_ln",
    )(x, wg, wu, wd, g, b)


def _in_proj_kernel(x_ref, wk_ref, wqvut_ref, k_ref, qvut_ref):
    xb = x_ref[...].astype(BF16)
    k_ref[...] = jnp.dot(xb, wk_ref[...], preferred_element_type=F32).astype(BF16)
    qvut = lax.dot_general(wqvut_ref[...], xb, (((1,), (1,)), ((), ())),
                           preferred_element_type=F32)
    qvut_ref[...] = qvut.astype(BF16)


def _in_proj(x, w_ku, w_qv_t):
    t, d = x.shape
    tm = min(PROJ_TM, t)
    n_ku, n_qv = w_ku.shape[1], w_qv_t.shape[0]
    return pl.pallas_call(
        _in_proj_kernel,
        grid=(t // tm,),
        in_specs=[
            pl.BlockSpec((tm, d), lambda i: (i, 0)),
            _resident((d, n_ku)),
            _resident((n_qv, d)),
        ],
        out_specs=[
            pl.BlockSpec((tm, n_ku), lambda i: (i, 0)),
            pl.BlockSpec((n_qv, tm), lambda i: (0, i)),
        ],
        out_shape=[jax.ShapeDtypeStruct((t, n_ku), BF16),
                   jax.ShapeDtypeStruct((n_qv, t), BF16)],
        compiler_params=_params("parallel"),
        name="in_proj",
    )(x, w_ku, w_qv_t)


def _diff_attn_kernel(qt_ref, k_ref, vt_ref, lam_ref, g_ref, o_ref,
                      acc_ref, m_ref, l_ref, ls_ref, smax_ref, *, tq, tk, lam_init):
    i = pl.program_id(2)
    dh = DIFF_HEAD_DIM
    ks, qs = ATTN_KS, ATTN_QS
    nkb = tq // tk
    n_strips, n_sub = 2 * tq // qs, tk // ks

    qt = qt_ref[...].astype(F32) * (dh ** -0.5 * LOG2_E)
    row = lax.broadcasted_iota(jnp.int32, qt.shape, 0)
    qz = jnp.concatenate([jnp.where(row < dh, qt, 0.0),
                          jnp.where(row >= dh, qt, 0.0)], axis=1).astype(BF16)

    def init():
        m_ref[...] = jnp.full(m_ref.shape, NEG_BIG, F32)
        l_ref[...] = jnp.zeros_like(l_ref)
        acc_ref[...] = jnp.zeros_like(acc_ref)
        ls_ref[...] = jnp.zeros_like(ls_ref)
        smax_ref[...] = jnp.full(smax_ref.shape, NEG_BIG, F32)

    def chunk_mask(shape, k_lo, q_lo):
        kc = (k_lo + lax.broadcasted_iota(jnp.int32, shape, 0)) // CHUNK
        qc = (q_lo + lax.broadcasted_iota(jnp.int32, shape, 1)) // CHUNK
        return kc <= qc

    def exact_step(j, koff):
        k0 = pl.multiple_of(j * tk, tk)
        kb = k_ref[pl.ds(k0, tk), :]
        s = jnp.dot(kb, qz, preferred_element_type=F32)
        if koff is not None:
            keep = chunk_mask((tk, tq), koff, 0)
            s = jnp.where(jnp.concatenate([keep, keep], axis=1), s, NEG_BIG)
        m_old = m_ref[...]
        m_new = jnp.maximum(m_old, jnp.max(s, axis=0, keepdims=True))
        corr = jnp.exp2(m_old - m_new)
        p = jnp.exp2(s - m_new)
        l_ref[...] = corr * l_ref[...] + jnp.sum(p, axis=0, keepdims=True)
        vtb = vt_ref[:, pl.ds(k0, tk)]
        acc_ref[...] = corr * acc_ref[...] + jnp.dot(vtb, p.astype(BF16),
                                                     preferred_element_type=F32)
        m_ref[...] = m_new

    def scores(j, koff, c, r):
        q_lo = (c * qs) % tq
        kb = k_ref[pl.ds(pl.multiple_of(j * tk, tk) + r * ks, ks), :]
        s = jnp.dot(kb, qz[:, c * qs:(c + 1) * qs], preferred_element_type=F32)
        if koff is not None and (koff + r * ks + ks - 1) // CHUNK > q_lo // CHUNK:
            s = jnp.where(chunk_mask(s.shape, koff + r * ks, q_lo), s, NEG_BIG)
        return s

    def prologue(koff):
        init()
        for c in range(n_strips):
            m_ref[:, c * qs:(c + 1) * qs] = jnp.max(scores(0, koff, c, 0), axis=0, keepdims=True)

    def fast_steps(blocks):
        def above_diagonal(c, b, r):
            koff = blocks[b][1]
            return koff is not None and (koff + r * ks) // CHUNK > ((c * qs) % tq + qs - 1) // CHUNK

        tiles = [(c, b, r) for c in range(n_strips) for b in range(len(blocks)) for r in range(n_sub)
                 if not above_diagonal(c, b, r)]
        score = lambda c, b, r: scores(blocks[b][0], blocks[b][1], c, r)
        in_flight = [score(*t) for t in tiles[:ATTN_LOOKAHEAD]]
        for n, (c, b, r) in enumerate(tiles):
            lanes = slice(c * qs, (c + 1) * qs)
            s = in_flight.pop(0)
            if n + ATTN_LOOKAHEAD < len(tiles):
                in_flight.append(score(*tiles[n + ATTN_LOOKAHEAD]))
            if n == 0 or tiles[n - 1][0] != c:
                shift = m_ref[:, lanes]
                lsum = ls_ref[:, lanes]
                smax = smax_ref[:, lanes]
                pv = None
            smax = jnp.maximum(smax, jnp.max(s.reshape(ks // 8, 8, qs), axis=0))
            e = jnp.exp2(s - shift)
            lsum = lsum + jnp.sum(e.reshape(ks // 8, 8, qs), axis=0)
            vtb = vt_ref[:, pl.ds(pl.multiple_of(blocks[b][0] * tk, tk) + r * ks, ks)]
            d = jnp.dot(vtb, e.astype(BF16), preferred_element_type=F32)
            pv = d if pv is None else pv + d
            if n + 1 == len(tiles) or tiles[n + 1][0] != c:
                acc_ref[:, lanes] += pv
                ls_ref[:, lanes] = lsum
                smax_ref[:, lanes] = smax

    diagonal = [(nkb * i + d, d * tk) for d in range(nkb)]

    def exact_body(j, carry):
        exact_step(j, None)
        return carry

    def fast_body(p, carry):
        fast_steps([(2 * nkb * p + d, None) for d in range(2 * nkb)])
        return carry

    @pl.when(i == 0)
    def _():
        prologue(0)

    @pl.when(i > 0)
    def _():
        prologue(None)
        lax.fori_loop(0, i // 2, fast_body, 0)

        @pl.when(i % 2 == 1)
        def _():
            fast_steps([(nkb * (i - 1) + d, None) for d in range(nkb)])

    fast_steps(diagonal)
    excess = jnp.max(smax_ref[...] - m_ref[...])

    @pl.when(excess > ATTN_SHIFT_SLACK)
    def _():
        init()
        lax.fori_loop(0, nkb * i, exact_body, 0)
        for j, koff in diagonal:
            exact_step(j, koff)

    lam_p = lam_ref[...]
    lam = (jnp.exp(jnp.sum(lam_p[0:1] * lam_p[1:2], axis=-1, keepdims=True))
           - jnp.exp(jnp.sum(lam_p[2:3] * lam_p[3:4], axis=-1, keepdims=True))
           + lam_init)
    acc = acc_ref[...]
    l = l_ref[...] + jnp.sum(ls_ref[...], axis=0, keepdims=True)
    ot = acc[:, :tq] / l[:, :tq] - lam * (acc[:, tq:] / l[:, tq:])
    o = ot.T
    o = o * lax.rsqrt(jnp.mean(o * o, axis=-1, keepdims=True) + RMS_EPS)
    o_ref[...] = (o * g_ref[...] * (1.0 - lam_init)).astype(o_ref.dtype)


def _diff_attn(qvt, ku, lam_p, g, lam_init, batch, seq):
    tk = min(ATTN_TK, seq)
    tq = min(ATTN_TQ, seq)
    assert tq % tk == 0 and tk % CHUNK == 0 and seq % tq == 0
    nq = seq // tq
    hh = DIFF_HEADS
    dv = DIFF_V_DIM
    kern = functools.partial(_diff_attn_kernel, tq=tq, tk=tk, lam_init=lam_init)
    return pl.pallas_call(
        kern,
        grid=(batch, hh, nq),
        in_specs=[
            pl.BlockSpec((dv, tq), lambda b, h, i: (h, b * nq + i)),
            pl.BlockSpec((seq, dv), lambda b, h, i: (b, h)),
            pl.BlockSpec((dv, seq), lambda b, h, i: (hh + h, b)),
            pl.BlockSpec((4, DIFF_HEAD_DIM), lambda b, h, i: (0, 0)),
            pl.BlockSpec((1, dv), lambda b, h, i: (0, 0)),
        ],
        out_specs=pl.BlockSpec((tq, dv), lambda b, h, i: (b * nq + i, h)),
        out_shape=jax.ShapeDtypeStruct((batch * seq, hh * dv), BF16),
        scratch_shapes=[pltpu.VMEM((dv, 2 * tq), F32),
                        pltpu.VMEM((1, 2 * tq), F32),
                        pltpu.VMEM((1, 2 * tq), F32),
                        pltpu.VMEM((8, 2 * tq), F32),
                        pltpu.VMEM((8, 2 * tq), F32)],
        compiler_params=_params("parallel", "parallel", "arbitrary"),
        name="diff_attn",
    )(qvt, ku, qvt, lam_p, g)


def _s5_kernel(ut_ref, taps_ref, gt_ref, ct_ref, apow_ref, y_ref, mt_ref, *, batch):
    L, P, N = S5_CHUNK, SSM_GROUP, SSM_STATE
    ut = ut_ref[0]
    cols = ut.shape[1]
    per_b = cols // batch

    keep = (lax.broadcasted_iota(jnp.int32, (L, P * L), 1) % L
            <= lax.broadcasted_iota(jnp.int32, (L, P * L), 0))
    for p in range(P):
        base = jnp.broadcast_to(taps_ref[0, p:p + 1, :], (L, P * L))
        rows = pltpu.roll(base, P * L - (L - 1), 1, stride=1, stride_axis=0)
        mt_ref[p * L:(p + 1) * L, :] = jnp.where(keep, rows, 0.0).astype(BF16)

    x = jnp.dot(gt_ref[0], ut, preferred_element_type=F32)
    x_re, x_im = x[:N], x[N:]
    chunk = lax.broadcasted_iota(jnp.int32, (N, cols), 1) % per_b
    for k in range(apow_ref.shape[1]):
        sh = 1 << k
        w_re, w_im = apow_ref[0, k, :, 0:1], apow_ref[0, k, :, 1:2]
        r_re = jnp.where(chunk >= sh, pltpu.roll(x_re, sh, 1), 0.0)
        r_im = jnp.where(chunk >= sh, pltpu.roll(x_im, sh, 1), 0.0)
        x_re, x_im = x_re + (w_re * r_re - w_im * r_im), x_im + (w_re * r_im + w_im * r_re)
    st = jnp.concatenate([jnp.where(chunk >= 1, pltpu.roll(x_re, 1, 1), 0.0),
                          jnp.where(chunk >= 1, pltpu.roll(x_im, 1, 1), 0.0)], axis=0)

    hi = st.astype(BF16)
    lo = (st - hi.astype(F32)).astype(BF16)
    ct = ct_ref[0]
    y = (jnp.dot(mt_ref[...], ut, preferred_element_type=F32)
         + jnp.dot(ct, hi, preferred_element_type=F32)
         + jnp.dot(ct, lo, preferred_element_type=F32))
    y_ref[0] = jax.nn.gelu(y).astype(y_ref.dtype)


def _s5_conv(ut_g, taps, g_t, c_t, a_pow, layer, batch):
    groups, lp, cols = ut_g.shape
    n = SSM_STATE
    kern = functools.partial(_s5_kernel, batch=batch)
    per_group = lambda *shape: pl.BlockSpec((1,) + shape, lambda g: (g,) + (0,) * len(shape))
    of_layer = lambda *shape: pl.BlockSpec((None, 1) + shape, lambda g: (layer, g) + (0,) * len(shape))
    return pl.pallas_call(
        kern,
        grid=(groups,),
        in_specs=[per_group(lp, cols), of_layer(SSM_GROUP, lp), of_layer(2 * n, lp),
                  of_layer(lp, 2 * n), of_layer(*a_pow.shape[2:])],
        out_specs=per_group(lp, cols),
        out_shape=jax.ShapeDtypeStruct((groups, lp, cols), BF16),
        scratch_shapes=[pltpu.VMEM((lp, lp), BF16)],
        compiler_params=_params("parallel"),
        name="s5_conv",
    )(ut_g, taps, g_t, c_t, a_pow)


def _s5_operators(lam_re, lam_im, log_step, b_re, b_im, c_re, c_im, d_skip, n_chunks):
    hp = lax.Precision.HIGHEST
    L, P, G, N = S5_CHUNK, SSM_GROUP, SSM_GROUPS, SSM_STATE
    lr = jnp.minimum(lam_re.astype(F32), -1e-4)
    li = lam_im.astype(F32)
    step = jnp.exp(log_step.astype(F32))[:, None]
    dr, di = lr * step, li * step
    tau = jnp.arange(L + 1, dtype=F32)[:, None, None]
    mag = jnp.exp(tau * dr)
    pw_re, pw_im = mag * jnp.cos(tau * di), mag * jnp.sin(tau * di)
    a_re, a_im = pw_re[1], pw_im[1]
    den = lr * lr + li * li
    f_re = ((a_re - 1.0) * lr + a_im * li) / den
    f_im = (a_im * lr - (a_re - 1.0) * li) / den
    br, bi = b_re.astype(F32), b_im.astype(F32)
    bb_re = f_re[..., None] * br - f_im[..., None] * bi
    bb_im = f_re[..., None] * bi + f_im[..., None] * br
    cr, ci = c_re.astype(F32), c_im.astype(F32)
    ca_re = cr[None] * pw_re[:, :, None, :] - ci[None] * pw_im[:, :, None, :]
    ca_im = cr[None] * pw_im[:, :, None, :] + ci[None] * pw_re[:, :, None, :]
    taps = (jnp.einsum('tgpn,gnq->gpqt', ca_re[:L], bb_re, precision=hp)
            - jnp.einsum('tgpn,gnq->gpqt', ca_im[:L], bb_im, precision=hp))
    skip = d_skip.astype(F32).reshape(G, P, 1) * jnp.eye(P, dtype=F32)
    taps = taps.at[..., 0].add(skip)
    taps = taps[..., ::-1].reshape(G, P, P * L)
    rev_re = pw_re[L - 1::-1].transpose(1, 2, 0)[:, :, None, :]
    rev_im = pw_im[L - 1::-1].transpose(1, 2, 0)[:, :, None, :]
    g_t = jnp.concatenate([rev_re * bb_re[..., None] - rev_im * bb_im[..., None],
                           rev_re * bb_im[..., None] + rev_im * bb_re[..., None]],
                          axis=1).reshape(G, 2 * N, P * L)
    c_t = jnp.concatenate([ca_re[1:].transpose(1, 2, 0, 3), -ca_im[1:].transpose(1, 2, 0, 3)],
                          axis=-1).reshape(G, P * L, 2 * N)
    n_steps = max(1, (n_chunks - 1).bit_length())
    span = (L * 2.0 ** jnp.arange(n_steps, dtype=F32))[None, :, None]
    mag = jnp.exp(span * dr[:, None, :])
    a_pow = jnp.stack([mag * jnp.cos(span * di[:, None, :]), mag * jnp.sin(span * di[:, None, :])], axis=-1)
    return taps, g_t.astype(BF16), c_t.astype(BF16), a_pow


def _mix_out_kernel(x_ref, attn_ref, y_ref, gw_ref, gb_ref, wo_ref, g_ref, b_ref, o_ref):
    yb = y_ref[...]
    z = jnp.dot(yb, gw_ref[...], preferred_element_type=F32) + gb_ref[...]
    y = yb.astype(F32) * jax.nn.sigmoid(z)
    aw = attn_ref.shape[1]
    mixed = (jnp.dot(attn_ref[...], wo_ref[:aw, :], preferred_element_type=F32)
             + jnp.dot(y.astype(BF16), wo_ref[aw:, :], preferred_element_type=F32))
    o_ref[...] = _layer_norm(ALPHA * x_ref[...] + mixed, g_ref[...], b_ref[...])


def _mix_out(x, attn, y_ssm, glu_w, glu_b, w_out, g, b):
    t, d = x.shape
    tm = min(PROJ_TM, t)
    aw, sw = attn.shape[1], y_ssm.shape[1]
    row = lambda i: (i, 0)
    return pl.pallas_call(
        _mix_out_kernel,
        grid=(t // tm,),
        in_specs=[
            pl.BlockSpec((tm, d), row),
            pl.BlockSpec((tm, aw), row),
            pl.BlockSpec((tm, sw), row),
            _resident((sw, sw)),
            _resident((1, sw)),
            _resident((d, d)),
            _resident((1, d)),
            _resident((1, d)),
        ],
        out_specs=pl.BlockSpec((tm, d), row),
        out_shape=jax.ShapeDtypeStruct((t, d), F32),
        compiler_params=_params("parallel"),
        name="mix_out",
    )(x, attn, y_ssm, glu_w, glu_b, w_out, g, b)


def _mem_fold_kernel(m_ref, wq_ref, wk_ref, wv_ref, wo_ref, qk_ref, vo_ref):
    mb = m_ref[0].astype(BF16)
    k = jnp.dot(mb, wk_ref[...], preferred_element_type=F32).astype(BF16)
    v = jnp.dot(mb, wv_ref[...], preferred_element_type=F32).astype(BF16)
    qk = lax.dot_general(wq_ref[...], k, (((1,), (1,)), ((), ())),
                         preferred_element_type=F32)
    qk_ref[0] = (qk * (XATTN_HEAD_DIM ** -0.5)).astype(BF16)
    vo_ref[0] = jnp.dot(v, wo_ref[...], preferred_element_type=F32).astype(BF16)


def _mem_fold(mem, wq, wk, wv, wo):
    batch, n_mem, d = mem.shape
    hd = XATTN_HEAD_DIM
    col = lambda b, h: (0, h)
    return pl.pallas_call(
        _mem_fold_kernel,
        grid=(batch, XATTN_HEADS),
        in_specs=[pl.BlockSpec((1, n_mem, d), lambda b, h: (b, 0, 0)),
                  pl.BlockSpec((d, hd), col),
                  pl.BlockSpec((d, hd), col),
                  pl.BlockSpec((d, hd), col),
                  pl.BlockSpec((hd, d), lambda b, h: (h, 0))],
        out_specs=[pl.BlockSpec((1, d, n_mem), lambda b, h: (b, 0, h)),
                   pl.BlockSpec((1, n_mem, d), lambda b, h: (b, h, 0))],
        out_shape=[jax.ShapeDtypeStruct((batch, d, XATTN_HEADS * n_mem), BF16),
                   jax.ShapeDtypeStruct((batch, XATTN_HEADS * n_mem, d), BF16)],
        compiler_params=_params("parallel", "parallel"),
        name="mem_fold",
    )(mem, wq, wk, wv, wo)


def _xattn_ln_kernel(x_ref, qk_ref, vo_ref, g_ref, b_ref, o_ref, *, n_mem):
    x = x_ref[...]
    s = jnp.dot(x.astype(BF16), qk_ref[0], preferred_element_type=F32)
    probs = []
    for h in range(XATTN_HEADS):
        sh = s[:, h * n_mem:(h + 1) * n_mem]
        e = jnp.exp(sh - jnp.max(sh, axis=-1, keepdims=True))
        probs.append((e / jnp.sum(e, axis=-1, keepdims=True)).astype(BF16))
    out = jnp.dot(jnp.concatenate(probs, axis=1), vo_ref[0], preferred_element_type=F32)
    o_ref[...] = _layer_norm(ALPHA * x + out, g_ref[...], b_ref[...])


def _xattn_ln(x, qk, vo, g, b, seq):
    t, d = x.shape
    tm = min(PROJ_TM, seq)
    per_b = seq // tm
    hm = qk.shape[2]
    kern = functools.partial(_xattn_ln_kernel, n_mem=hm // XATTN_HEADS)
    return pl.pallas_call(
        kern,
        grid=(t // tm,),
        in_specs=[
            pl.BlockSpec((tm, d), lambda i: (i, 0)),
            pl.BlockSpec((1, d, hm), lambda i: (i // per_b, 0, 0)),
            pl.BlockSpec((1, hm, d), lambda i: (i // per_b, 0, 0)),
            _resident((1, d)),
            _resident((1, d)),
        ],
        out_specs=pl.BlockSpec((tm, d), lambda i: (i, 0)),
        out_shape=jax.ShapeDtypeStruct((t, d), F32),
        compiler_params=_params("parallel"),
        name="xattn_ln",
    )(x, qk, vo, g, b)


def kernel(x, mem, ffn1_w_gate, ffn1_w_up, ffn1_w_down, ln1_g, ln1_b, w_in, lambda_q1, lambda_k1, lambda_q2, lambda_k2, diff_norm_g, ssm_lambda_re, ssm_lambda_im, ssm_log_step, ssm_b_re, ssm_b_im, ssm_c_re, ssm_c_im, ssm_d, ssm_glu_w, ssm_glu_b, w_out, ln2_g, ln2_b, xattn_w_q, xattn_w_k, xattn_w_v, xattn_w_o, ln3_g, ln3_b, ffn2_w_gate, ffn2_w_up, ffn2_w_down, ln4_g, ln4_b):
    batch, seq, d = x.shape
    t = batch * seq
    aw = ATTN_WIDTH
    L, P, G = S5_CHUNK, SSM_GROUP, SSM_GROUPS
    n_chunks = seq // L
    xf = x.reshape(t, d)
    row = lambda a: a.reshape(1, -1).astype(F32)
    s5_ops = jax.vmap(functools.partial(_s5_operators, n_chunks=n_chunks))(
        ssm_lambda_re, ssm_lambda_im, ssm_log_step, ssm_b_re, ssm_b_im, ssm_c_re, ssm_c_im, ssm_d)

    for l in range(DEPTH):
        lam_init = 0.8 - 0.6 * math.exp(-0.3 * l)
        xf = _ffn_ln(xf, ffn1_w_gate[l].astype(BF16), ffn1_w_up[l].astype(BF16),
                     ffn1_w_down[l].astype(BF16), row(ln1_g[l]), row(ln1_b[l]))

        w = w_in[l]
        w_k = w[:, aw:2 * aw].astype(BF16)
        w_qvu_t = jnp.concatenate([w[:, :aw], w[:, 2 * aw:]], axis=1).T.astype(BF16)
        k_nat, qvut = _in_proj(xf, w_k, w_qvu_t)

        lam_p = jnp.stack([lambda_q1[l], lambda_k1[l], lambda_q2[l], lambda_k2[l]]).astype(F32)
        attn = _diff_attn(qvut, k_nat, lam_p, row(diff_norm_g[l]), lam_init, batch, seq)

        ut_g = (qvut[2 * aw:].reshape(G * P, batch * n_chunks, L)
                .swapaxes(1, 2).reshape(G, P * L, batch * n_chunks))
        yt_g = _s5_conv(ut_g, *s5_ops, l, batch)
        y_ssm = yt_g.reshape(G * P, L, batch * n_chunks).transpose(2, 1, 0).reshape(t, G * P)

        xf = _mix_out(xf, attn, y_ssm, ssm_glu_w[l].astype(BF16),
                      row(ssm_glu_b[l]), w_out[l].astype(BF16), row(ln2_g[l]), row(ln2_b[l]))

        qk, vo = _mem_fold(mem, xattn_w_q[l].astype(BF16), xattn_w_k[l].astype(BF16),
                           xattn_w_v[l].astype(BF16), xattn_w_o[l].astype(BF16))
        xf = _xattn_ln(xf, qk, vo, row(ln3_g[l]), row(ln3_b[l]), seq)

        xf = _ffn_ln(xf, ffn2_w_gate[l].astype(BF16), ffn2_w_up[l].astype(BF16),
                     ffn2_w_down[l].astype(BF16), row(ln4_g[l]), row(ln4_b[l]))
    return xf.reshape(batch, seq, d)
```

```python
import functools
import math

import jax
import jax.numpy as jnp
from jax import lax
from jax.experimental import pallas as pl
from jax.experimental.pallas import tpu as pltpu

F32 = jnp.float32
BF16 = jnp.bfloat16

D_MODEL = 2048
DEPTH = 4
CHUNK = 64
ATTN_WIDTH = D_MODEL // 2
SSM_WIDTH = D_MODEL - ATTN_WIDTH
DIFF_HEAD_DIM = 64
DIFF_V_DIM = 2 * DIFF_HEAD_DIM
DIFF_HEADS = ATTN_WIDTH // DIFF_V_DIM
SSM_GROUP = 16
SSM_GROUPS = SSM_WIDTH // SSM_GROUP
SSM_STATE = 64
XATTN_HEADS = 4
XATTN_HEAD_DIM = D_MODEL // XATTN_HEADS
ALPHA = (2 * DEPTH) ** 0.25
LN_EPS = 1e-5
RMS_EPS = 1e-5
NEG_BIG = -1e30
LOG2_E = 1.4426950408889634

V7X_VMEM_LIMIT_BYTES = 56 * 1024 * 1024

FFN_TM = 512
FFN_TF = 512
PROJ_TM = 512
ATTN_TQ = 1024
ATTN_TK = 512
ATTN_KS = 256
ATTN_QS = 512
ATTN_LOOKAHEAD = 2
ATTN_SHIFT_SLACK = 64.0
S5_CHUNK = 64


def _params(*sem):
    return pltpu.CompilerParams(dimension_semantics=sem,
                                vmem_limit_bytes=V7X_VMEM_LIMIT_BYTES)


def _resident(shape):
    return pl.BlockSpec(shape, lambda *_: (0,) * len(shape), pipeline_mode=pl.Buffered(1))


def _layer_norm(y, g, b):
    mu = jnp.mean(y, axis=-1, keepdims=True)
    yc = y - mu
    var = jnp.mean(yc * yc, axis=-1, keepdims=True)
    return yc * lax.rsqrt(var + LN_EPS) * g + b


def _ffn_ln_kernel(x_ref, wg_ref, wu_ref, wd_ref, g_ref, b_ref, o_ref, xb_ref, acc_ref):
    f = pl.program_id(1)

    @pl.when(f == 0)
    def _():
        xb_ref[...] = x_ref[...].astype(BF16)
        acc_ref[...] = jnp.zeros_like(acc_ref)

    xb = xb_ref[...]
    gate = jnp.dot(xb, wg_ref[...], preferred_element_type=F32)
    up = jnp.dot(xb, wu_ref[...], preferred_element_type=F32)
    h = (gate * jax.nn.sigmoid(gate) * up).astype(BF16)
    acc_ref[...] += jnp.dot(h, wd_ref[...], preferred_element_type=F32)

    @pl.when(f == pl.num_programs(1) - 1)
    def _():
        y = ALPHA * x_ref[...] + 0.5 * acc_ref[...]
        o_ref[...] = _layer_norm(y, g_ref[...], b_ref[...])


def _ffn_ln(x, wg, wu, wd, g, b):
    t, d = x.shape
    f = wg.shape[1]
    tm, tf = min(FFN_TM, t), FFN_TF
    return pl.pallas_call(
        _ffn_ln_kernel,
        grid=(t // tm, f // tf),
        in_specs=[
            pl.BlockSpec((tm, d), lambda i, j: (i, 0)),
            pl.BlockSpec((d, tf), lambda i, j: (0, j)),
            pl.BlockSpec((d, tf), lambda i, j: (0, j)),
            pl.BlockSpec((tf, d), lambda i, j: (j, 0)),
            pl.BlockSpec((1, d), lambda i, j: (0, 0)),
            pl.BlockSpec((1, d), lambda i, j: (0, 0)),
        ],
        out_specs=pl.BlockSpec((tm, d), lambda i, j: (i, 0)),
        out_shape=jax.ShapeDtypeStruct((t, d), F32),
        scratch_shapes=[pltpu.VMEM((tm, d), BF16), pltpu.VMEM((tm, d), F32)],
        compiler_params=_params("parallel", "arbitrary"),
        name="ffn_ln",
    )(x, wg, wu, wd, g, b)


def _in_proj_kernel(x_ref, wk_ref, wqvut_ref, k_ref, qvut_ref):
    xb = x_ref[...].astype(BF16)
    k_ref[...] = jnp.dot(xb, wk_ref[...], preferred_element_type=F32).astype(BF16)
    qvut = lax.dot_general(wqvut_ref[...], xb, (((1,), (1,)), ((), ())),
                           preferred_element_type=F32)
    qvut_ref[...] = qvut.astype(BF16)


def _in_proj(x, w_ku, w_qv_t):
    t, d = x.shape
    tm = min(PROJ_TM, t)
    n_ku, n_qv = w_ku.shape[1], w_qv_t.shape[0]
    return pl.pallas_call(
        _in_proj_kernel,
        grid=(t // tm,),
        in_specs=[
            pl.BlockSpec((tm, d), lambda i: (i, 0)),
            _resident((d, n_ku)),
            _resident((n_qv, d)),
        ],
        out_specs=[
            pl.BlockSpec((tm, n_ku), lambda i: (i, 0)),
            pl.BlockSpec((n_qv, tm), lambda i: (0, i)),
        ],
        out_shape=[jax.ShapeDtypeStruct((t, n_ku), BF16),
                   jax.ShapeDtypeStruct((n_qv, t), BF16)],
        compiler_params=_params("parallel"),
        name="in_proj",
    )(x, w_ku, w_qv_t)


def _diff_attn_kernel(qt_ref, k_ref, vt_ref, lam_ref, g_ref, o_ref,
                      acc_ref, m_ref, l_ref, ls_ref, *, tq, tk, lam_init):
    i = pl.program_id(2)
    dh = DIFF_HEAD_DIM
    ks, qs = ATTN_KS, ATTN_QS
    nkb = tq // tk
    n_strips, n_sub = 2 * tq // qs, tk // ks

    qt = qt_ref[...].astype(F32) * (dh ** -0.5 * LOG2_E)
    row = lax.broadcasted_iota(jnp.int32, qt.shape, 0)
    qz = jnp.concatenate([jnp.where(row < dh, qt, 0.0),
                          jnp.where(row >= dh, qt, 0.0)], axis=1).astype(BF16)

    def init():
        m_ref[...] = jnp.full(m_ref.shape, NEG_BIG, F32)
        l_ref[...] = jnp.zeros_like(l_ref)
        acc_ref[...] = jnp.zeros_like(acc_ref)
        ls_ref[...] = jnp.zeros_like(ls_ref)

    def chunk_mask(shape, k_lo, q_lo):
        kc = (k_lo + lax.broadcasted_iota(jnp.int32, shape, 0)) // CHUNK
        qc = (q_lo + lax.broadcasted_iota(jnp.int32, shape, 1)) // CHUNK
        return kc <= qc

    def exact_step(j, koff):
        k0 = pl.multiple_of(j * tk, tk)
        kb = k_ref[pl.ds(k0, tk), :]
        s = jnp.dot(kb, qz, preferred_element_type=F32)
        if koff is not None:
            keep = chunk_mask((tk, tq), koff, 0)
            s = jnp.where(jnp.concatenate([keep, keep], axis=1), s, NEG_BIG)
        m_old = m_ref[...]
        m_new = jnp.maximum(m_old, jnp.max(s, axis=0, keepdims=True))
        corr = jnp.exp2(m_old - m_new)
        p = jnp.exp2(s - m_new)
        l_ref[...] = corr * l_ref[...] + jnp.sum(p, axis=0, keepdims=True)
        vtb = vt_ref[:, pl.ds(k0, tk)]
        acc_ref[...] = corr * acc_ref[...] + jnp.dot(vtb, p.astype(BF16),
                                                     preferred_element_type=F32)
        m_ref[...] = m_new

    def scores(j, koff, c, r):
        q_lo = (c * qs) % tq
        kb = k_ref[pl.ds(pl.multiple_of(j * tk, tk) + r * ks, ks), :]
        s = jnp.dot(kb, qz[:, c * qs:(c + 1) * qs], preferred_element_type=F32)
        if koff is not None and (koff + r * ks + ks - 1) // CHUNK > q_lo // CHUNK:
            s = jnp.where(chunk_mask(s.shape, koff + r * ks, q_lo), s, NEG_BIG)
        return s

    def prologue(koff):
        init()
        for c in range(n_strips):
            m_ref[:, c * qs:(c + 1) * qs] = jnp.max(scores(0, koff, c, 0), axis=0, keepdims=True)

    def fast_steps(blocks):
        def above_diagonal(c, b, r):
            koff = blocks[b][1]
            return koff is not None and (koff + r * ks) // CHUNK > ((c * qs) % tq + qs - 1) // CHUNK

        tiles = [(c, b, r) for c in range(n_strips) for b in range(len(blocks)) for r in range(n_sub)
                 if not above_diagonal(c, b, r)]
        score = lambda c, b, r: scores(blocks[b][0], blocks[b][1], c, r)
        in_flight = [score(*t) for t in tiles[:ATTN_LOOKAHEAD]]
        for n, (c, b, r) in enumerate(tiles):
            lanes = slice(c * qs, (c + 1) * qs)
            s = in_flight.pop(0)
            if n + ATTN_LOOKAHEAD < len(tiles):
                in_flight.append(score(*tiles[n + ATTN_LOOKAHEAD]))
            if n == 0 or tiles[n - 1][0] != c:
                shift = m_ref[:, lanes]
                lsum = ls_ref[:, lanes]
                pv = None
            e = jnp.exp2(s - shift)
            lsum = lsum + jnp.sum(e.reshape(ks // 8, 8, qs), axis=0)
            vtb = vt_ref[:, pl.ds(pl.multiple_of(blocks[b][0] * tk, tk) + r * ks, ks)]
            d = jnp.dot(vtb, e.astype(BF16), preferred_element_type=F32)
            pv = d if pv is None else pv + d
            if n + 1 == len(tiles) or tiles[n + 1][0] != c:
                acc_ref[:, lanes] += pv
                ls_ref[:, lanes] = lsum

    diagonal = [(nkb * i + d, d * tk) for d in range(nkb)]

    def exact_body(j, carry):
        exact_step(j, None)
        return carry

    def fast_body(p, carry):
        fast_steps([(2 * nkb * p + d, None) for d in range(2 * nkb)])
        return carry

    @pl.when(i == 0)
    def _():
        prologue(0)

    @pl.when(i > 0)
    def _():
        prologue(None)
        lax.fori_loop(0, i // 2, fast_body, 0)

        @pl.when(i % 2 == 1)
        def _():
            fast_steps([(nkb * (i - 1) + d, None) for d in range(nkb)])

    fast_steps(diagonal)
    in_range = jnp.max(ls_ref[...]) <= 2.0 ** ATTN_SHIFT_SLACK

    @pl.when(jnp.logical_not(in_range))
    def _():
        init()
        lax.fori_loop(0, nkb * i, exact_body, 0)
        for j, koff in diagonal:
            exact_step(j, koff)

    lam_p = lam_ref[...]
    lam = (jnp.exp(jnp.sum(lam_p[0:1] * lam_p[1:2], axis=-1, keepdims=True))
           - jnp.exp(jnp.sum(lam_p[2:3] * lam_p[3:4], axis=-1, keepdims=True))
           + lam_init)
    acc = acc_ref[...]
    l = l_ref[...] + jnp.sum(ls_ref[...], axis=0, keepdims=True)
    ot = acc[:, :tq] / l[:, :tq] - lam * (acc[:, tq:] / l[:, tq:])
    o = ot.T
    o = o * lax.rsqrt(jnp.mean(o * o, axis=-1, keepdims=True) + RMS_EPS)
    o_ref[...] = (o * g_ref[...] * (1.0 - lam_init)).astype(o_ref.dtype)


def _diff_attn(qvt, ku, lam_p, g, lam_init, batch, seq):
    tk = min(ATTN_TK, seq)
    tq = min(ATTN_TQ, seq)
    assert tq % tk == 0 and tk % CHUNK == 0 and seq % tq == 0
    nq = seq // tq
    hh = DIFF_HEADS
    dv = DIFF_V_DIM
    kern = functools.partial(_diff_attn_kernel, tq=tq, tk=tk, lam_init=lam_init)
    return pl.pallas_call(
        kern,
        grid=(batch, hh, nq),
        in_specs=[
            pl.BlockSpec((dv, tq), lambda b, h, i: (h, b * nq + i)),
            pl.BlockSpec((seq, dv), lambda b, h, i: (b, h)),
            pl.BlockSpec((dv, seq), lambda b, h, i: (hh + h, b)),
            pl.BlockSpec((4, DIFF_HEAD_DIM), lambda b, h, i: (0, 0)),
            pl.BlockSpec((1, dv), lambda b, h, i: (0, 0)),
        ],
        out_specs=pl.BlockSpec((tq, dv), lambda b, h, i: (b * nq + i, h)),
        out_shape=jax.ShapeDtypeStruct((batch * seq, hh * dv), BF16),
        scratch_shapes=[pltpu.VMEM((dv, 2 * tq), F32),
                        pltpu.VMEM((1, 2 * tq), F32),
                        pltpu.VMEM((1, 2 * tq), F32),
                        pltpu.VMEM((8, 2 * tq), F32)],
        compiler_params=_params("parallel", "parallel", "arbitrary"),
        name="diff_attn",
    )(qvt, ku, qvt, lam_p, g)


def _s5_kernel(ut_ref, taps_ref, gt_ref, ct_ref, apow_ref, y_ref, mt_ref, *, batch):
    L, P, N = S5_CHUNK, SSM_GROUP, SSM_STATE
    ut = ut_ref[0]
    cols = ut.shape[1]
    per_b = cols // batch

    keep = (lax.broadcasted_iota(jnp.int32, (L, P * L), 1) % L
            <= lax.broadcasted_iota(jnp.int32, (L, P * L), 0))
    for p in range(P):
        base = jnp.broadcast_to(taps_ref[0, p:p + 1, :], (L, P * L))
        rows = pltpu.roll(base, P * L - (L - 1), 1, stride=1, stride_axis=0)
        mt_ref[p * L:(p + 1) * L, :] = jnp.where(keep, rows, 0.0).astype(BF16)

    x = jnp.dot(gt_ref[0], ut, preferred_element_type=F32)
    x_re, x_im = x[:N], x[N:]
    chunk = lax.broadcasted_iota(jnp.int32, (N, cols), 1) % per_b
    for k in range(apow_ref.shape[1]):
        sh = 1 << k
        w_re, w_im = apow_ref[0, k, :, 0:1], apow_ref[0, k, :, 1:2]
        r_re = jnp.where(chunk >= sh, pltpu.roll(x_re, sh, 1), 0.0)
        r_im = jnp.where(chunk >= sh, pltpu.roll(x_im, sh, 1), 0.0)
        x_re, x_im = x_re + (w_re * r_re - w_im * r_im), x_im + (w_re * r_im + w_im * r_re)
    st = jnp.concatenate([jnp.where(chunk >= 1, pltpu.roll(x_re, 1, 1), 0.0),
                          jnp.where(chunk >= 1, pltpu.roll(x_im, 1, 1), 0.0)], axis=0)

    hi = st.astype(BF16)
    lo = (st - hi.astype(F32)).astype(BF16)
    ct = ct_ref[0]
    y = (jnp.dot(mt_ref[...], ut, preferred_element_type=F32)
         + jnp.dot(ct, hi, preferred_element_type=F32)
         + jnp.dot(ct, lo, preferred_element_type=F32))
    y_ref[0] = jax.nn.gelu(y).astype(y_ref.dtype)


def _s5_conv(ut_g, taps, g_t, c_t, a_pow, layer, batch):
    groups, lp, cols = ut_g.shape
    n = SSM_STATE
    kern = functools.partial(_s5_kernel, batch=batch)
    per_group = lambda *shape: pl.BlockSpec((1,) + shape, lambda g: (g,) + (0,) * len(shape))
    of_layer = lambda *shape: pl.BlockSpec((None, 1) + shape, lambda g: (layer, g) + (0,) * len(shape))
    return pl.pallas_call(
        kern,
        grid=(groups,),
        in_specs=[per_group(lp, cols), of_layer(SSM_GROUP, lp), of_layer(2 * n, lp),
                  of_layer(lp, 2 * n), of_layer(*a_pow.shape[2:])],
        out_specs=per_group(lp, cols),
        out_shape=jax.ShapeDtypeStruct((groups, lp, cols), BF16),
        scratch_shapes=[pltpu.VMEM((lp, lp), BF16)],
        compiler_params=_params("parallel"),
        name="s5_conv",
    )(ut_g, taps, g_t, c_t, a_pow)


def _s5_operators(lam_re, lam_im, log_step, b_re, b_im, c_re, c_im, d_skip, n_chunks):
    hp = lax.Precision.HIGHEST
    L, P, G, N = S5_CHUNK, SSM_GROUP, SSM_GROUPS, SSM_STATE
    lr = jnp.minimum(lam_re.astype(F32), -1e-4)
    li = lam_im.astype(F32)
    step = jnp.exp(log_step.astype(F32))[:, None]
    dr, di = lr * step, li * step
    tau = jnp.arange(L + 1, dtype=F32)
    mag = jnp.exp(dr[..., None] * tau)
    pw_re, pw_im = mag * jnp.cos(di[..., None] * tau), mag * jnp.sin(di[..., None] * tau)
    a_re, a_im = pw_re[..., 1], pw_im[..., 1]
    den = lr * lr + li * li
    f_re = ((a_re - 1.0) * lr + a_im * li) / den
    f_im = (a_im * lr - (a_re - 1.0) * li) / den
    br, bi = b_re.astype(F32), b_im.astype(F32)
    bb_re = f_re[..., None] * br - f_im[..., None] * bi
    bb_im = f_re[..., None] * bi + f_im[..., None] * br
    cr, ci = c_re.astype(F32)[:, :, None, :], c_im.astype(F32)[:, :, None, :]
    bq_re, bq_im = bb_re.transpose(0, 2, 1)[:, None], bb_im.transpose(0, 2, 1)[:, None]
    taps = (jnp.einsum('gpqn,gnt->gpqt', cr * bq_re - ci * bq_im, pw_re[..., :L], precision=hp)
            - jnp.einsum('gpqn,gnt->gpqt', cr * bq_im + ci * bq_re, pw_im[..., :L], precision=hp))
    skip = d_skip.astype(F32).reshape(G, P, 1) * jnp.eye(P, dtype=F32)
    taps = taps.at[..., 0].add(skip)
    taps = taps[..., ::-1].reshape(G, P, P * L)
    rev_re, rev_im = pw_re[:, :, None, L - 1::-1], pw_im[:, :, None, L - 1::-1]
    g_t = jnp.concatenate([rev_re * bb_re[..., None] - rev_im * bb_im[..., None],
                           rev_re * bb_im[..., None] + rev_im * bb_re[..., None]],
                          axis=1).reshape(G, 2 * N, P * L)
    nx_re = pw_re[..., 1:].transpose(0, 2, 1)[:, None]
    nx_im = pw_im[..., 1:].transpose(0, 2, 1)[:, None]
    c_t = jnp.concatenate([cr * nx_re - ci * nx_im, -(cr * nx_im + ci * nx_re)],
                          axis=-1).reshape(G, P * L, 2 * N)
    n_steps = max(1, (n_chunks - 1).bit_length())
    span = (L * 2.0 ** jnp.arange(n_steps, dtype=F32))[None, :, None]
    mag = jnp.exp(span * dr[:, None, :])
    a_pow = jnp.stack([mag * jnp.cos(span * di[:, None, :]), mag * jnp.sin(span * di[:, None, :])], axis=-1)
    return taps, g_t.astype(BF16), c_t.astype(BF16), a_pow


def _mix_out_kernel(x_ref, attn_ref, y_ref, gw_ref, gb_ref, wo_ref, g_ref, b_ref, o_ref):
    yb = y_ref[...]
    z = jnp.dot(yb, gw_ref[...], preferred_element_type=F32) + gb_ref[...]
    y = yb.astype(F32) * jax.nn.sigmoid(z)
    aw = attn_ref.shape[1]
    mixed = (jnp.dot(attn_ref[...], wo_ref[:aw, :], preferred_element_type=F32)
             + jnp.dot(y.astype(BF16), wo_ref[aw:, :], preferred_element_type=F32))
    o_ref[...] = _layer_norm(ALPHA * x_ref[...] + mixed, g_ref[...], b_ref[...])


def _mix_out(x, attn, y_ssm, glu_w, glu_b, w_out, g, b):
    t, d = x.shape
    tm = min(PROJ_TM, t)
    aw, sw = attn.shape[1], y_ssm.shape[1]
    row = lambda i: (i, 0)
    return pl.pallas_call(
        _mix_out_kernel,
        grid=(t // tm,),
        in_specs=[
            pl.BlockSpec((tm, d), row),
            pl.BlockSpec((tm, aw), row),
            pl.BlockSpec((tm, sw), row),
            _resident((sw, sw)),
            _resident((1, sw)),
            _resident((d, d)),
            _resident((1, d)),
            _resident((1, d)),
        ],
        out_specs=pl.BlockSpec((tm, d), row),
        out_shape=jax.ShapeDtypeStruct((t, d), F32),
        compiler_params=_params("parallel"),
        name="mix_out",
    )(x, attn, y_ssm, glu_w, glu_b, w_out, g, b)


def _mem_fold_kernel(m_ref, wq_ref, wk_ref, wv_ref, wo_ref, qk_ref, vo_ref):
    mb = m_ref[0].astype(BF16)
    k = jnp.dot(mb, wk_ref[...], preferred_element_type=F32).astype(BF16)
    v = jnp.dot(mb, wv_ref[...], preferred_element_type=F32).astype(BF16)
    qk = lax.dot_general(wq_ref[...], k, (((1,), (1,)), ((), ())),
                         preferred_element_type=F32)
    qk_ref[0] = (qk * (XATTN_HEAD_DIM ** -0.5)).astype(BF16)
    vo_ref[0] = jnp.dot(v, wo_ref[...], preferred_element_type=F32).astype(BF16)


def _mem_fold(mem, wq, wk, wv, wo):
    batch, n_mem, d = mem.shape
    hd = XATTN_HEAD_DIM
    col = lambda b, h: (0, h)
    return pl.pallas_call(
        _mem_fold_kernel,
        grid=(batch, XATTN_HEADS),
        in_specs=[pl.BlockSpec((1, n_mem, d), lambda b, h: (b, 0, 0)),
                  pl.BlockSpec((d, hd), col),
                  pl.BlockSpec((d, hd), col),
                  pl.BlockSpec((d, hd), col),
                  pl.BlockSpec((hd, d), lambda b, h: (h, 0))],
        out_specs=[pl.BlockSpec((1, d, n_mem), lambda b, h: (b, 0, h)),
                   pl.BlockSpec((1, n_mem, d), lambda b, h: (b, h, 0))],
        out_shape=[jax.ShapeDtypeStruct((batch, d, XATTN_HEADS * n_mem), BF16),
                   jax.ShapeDtypeStruct((batch, XATTN_HEADS * n_mem, d), BF16)],
        compiler_params=_params("parallel", "parallel"),
        name="mem_fold",
    )(mem, wq, wk, wv, wo)


def _xattn_ln_kernel(x_ref, qk_ref, vo_ref, g_ref, b_ref, o_ref, *, n_mem):
    x = x_ref[...]
    s = jnp.dot(x.astype(BF16), qk_ref[0], preferred_element_type=F32)
    probs = []
    for h in range(XATTN_HEADS):
        sh = s[:, h * n_mem:(h + 1) * n_mem]
        e = jnp.exp(sh - jnp.max(sh, axis=-1, keepdims=True))
        probs.append((e / jnp.sum(e, axis=-1, keepdims=True)).astype(BF16))
    out = jnp.dot(jnp.concatenate(probs, axis=1), vo_ref[0], preferred_element_type=F32)
    o_ref[...] = _layer_norm(ALPHA * x + out, g_ref[...], b_ref[...])


def _xattn_ln(x, qk, vo, g, b, seq):
    t, d = x.shape
    tm = min(PROJ_TM, seq)
    per_b = seq // tm
    hm = qk.shape[2]
    kern = functools.partial(_xattn_ln_kernel, n_mem=hm // XATTN_HEADS)
    return pl.pallas_call(
        kern,
        grid=(t // tm,),
        in_specs=[
            pl.BlockSpec((tm, d), lambda i: (i, 0)),
            pl.BlockSpec((1, d, hm), lambda i: (i // per_b, 0, 0)),
            pl.BlockSpec((1, hm, d), lambda i: (i // per_b, 0, 0)),
            _resident((1, d)),
            _resident((1, d)),
        ],
        out_specs=pl.BlockSpec((tm, d), lambda i: (i, 0)),
        out_shape=jax.ShapeDtypeStruct((t, d), F32),
        compiler_params=_params("parallel"),
        name="xattn_ln",
    )(x, qk, vo, g, b)


def kernel(x, mem, ffn1_w_gate, ffn1_w_up, ffn1_w_down, ln1_g, ln1_b, w_in, lambda_q1, lambda_k1, lambda_q2, lambda_k2, diff_norm_g, ssm_lambda_re, ssm_lambda_im, ssm_log_step, ssm_b_re, ssm_b_im, ssm_c_re, ssm_c_im, ssm_d, ssm_glu_w, ssm_glu_b, w_out, ln2_g, ln2_b, xattn_w_q, xattn_w_k, xattn_w_v, xattn_w_o, ln3_g, ln3_b, ffn2_w_gate, ffn2_w_up, ffn2_w_down, ln4_g, ln4_b):
    batch, seq, d = x.shape
    t = batch * seq
    aw = ATTN_WIDTH
    L, P, G = S5_CHUNK, SSM_GROUP, SSM_GROUPS
    n_chunks = seq // L
    xf = x.reshape(t, d)
    row = lambda a: a.reshape(1, -1).astype(F32)
    s5_ops = jax.vmap(functools.partial(_s5_operators, n_chunks=n_chunks))(
        ssm_lambda_re, ssm_lambda_im, ssm_log_step, ssm_b_re, ssm_b_im, ssm_c_re, ssm_c_im, ssm_d)

    for l in range(DEPTH):
        lam_init = 0.8 - 0.6 * math.exp(-0.3 * l)
        xf = _ffn_ln(xf, ffn1_w_gate[l].astype(BF16), ffn1_w_up[l].astype(BF16),
                     ffn1_w_down[l].astype(BF16), row(ln1_g[l]), row(ln1_b[l]))

        w = w_in[l]
        w_k = w[:, aw:2 * aw].astype(BF16)
        w_qvu_t = jnp.concatenate([w[:, :aw], w[:, 2 * aw:]], axis=1).T.astype(BF16)
        k_nat, qvut = _in_proj(xf, w_k, w_qvu_t)

        lam_p = jnp.stack([lambda_q1[l], lambda_k1[l], lambda_q2[l], lambda_k2[l]]).astype(F32)
        attn = _diff_attn(qvut, k_nat, lam_p, row(diff_norm_g[l]), lam_init, batch, seq)

        ut_g = (qvut[2 * aw:].reshape(G * P, batch * n_chunks, L)
                .swapaxes(1, 2).reshape(G, P * L, batch * n_chunks))
        yt_g = _s5_conv(ut_g, *s5_ops, l, batch)
        y_ssm = yt_g.reshape(G * P, L, batch * n_chunks).transpose(2, 1, 0).reshape(t, G * P)

        xf = _mix_out(xf, attn, y_ssm, ssm_glu_w[l].astype(BF16),
                      row(ssm_glu_b[l]), w_out[l].astype(BF16), row(ln2_g[l]), row(ln2_b[l]))

        qk, vo = _mem_fold(mem, xattn_w_q[l].astype(BF16), xattn_w_k[l].astype(BF16),
                           xattn_w_v[l].astype(BF16), xattn_w_o[l].astype(BF16))
        xf = _xattn_ln(xf, qk, vo, row(ln3_g[l]), row(ln3_b[l]), seq)

        xf = _ffn_ln(xf, ffn2_w_gate[l].astype(BF16), ffn2_w_up[l].astype(BF16),
                     ffn2_w_down[l].astype(BF16), row(ln4_g[l]), row(ln4_b[l]))
    return xf.reshape(batch, seq, d)
```

```python
import functools
import math

import jax
import jax.numpy as jnp
from jax import lax
from jax.experimental import pallas as pl
from jax.experimental.pallas import tpu as pltpu

F32 = jnp.float32
BF16 = jnp.bfloat16

D_MODEL = 2048
DEPTH = 4
CHUNK = 64
ATTN_WIDTH = D_MODEL // 2
SSM_WIDTH = D_MODEL - ATTN_WIDTH
DIFF_HEAD_DIM = 64
DIFF_V_DIM = 2 * DIFF_HEAD_DIM
DIFF_HEADS = ATTN_WIDTH // DIFF_V_DIM
SSM_GROUP = 16
SSM_GROUPS = SSM_WIDTH // SSM_GROUP
SSM_STATE = 64
XATTN_HEADS = 4
XATTN_HEAD_DIM = D_MODEL // XATTN_HEADS
ALPHA = (2 * DEPTH) ** 0.25
LN_EPS = 1e-5
RMS_EPS = 1e-5
NEG_BIG = -1e30
LOG2_E = 1.4426950408889634

V7X_VMEM_LIMIT_BYTES = 56 * 1024 * 1024

FFN_TM = 512
FFN_TF = 512
PROJ_TM = 512
EPILOGUE_SLABS = 2
ATTN_TQ = 1024
ATTN_TK = 512
ATTN_KS = 256
ATTN_QS = 512
ATTN_LOOKAHEAD = 2
ATTN_SHIFT_SLACK = 64.0
S5_CHUNK = 64


def _params(*sem):
    return pltpu.CompilerParams(dimension_semantics=sem,
                                vmem_limit_bytes=V7X_VMEM_LIMIT_BYTES)


def _resident(shape):
    return pl.BlockSpec(shape, lambda *_: (0,) * len(shape), pipeline_mode=pl.Buffered(1))


def _layer_norm(y, g, b):
    mu = jnp.mean(y, axis=-1, keepdims=True)
    yc = y - mu
    var = jnp.mean(yc * yc, axis=-1, keepdims=True)
    return yc * lax.rsqrt(var + LN_EPS) * g + b


def _ffn_ln_kernel(x_ref, wg_ref, wu_ref, wd_ref, g_ref, b_ref, o_ref, xb_ref, acc_ref):
    f = pl.program_id(1)
    last = pl.num_programs(1) - 1

    @pl.when(f == 0)
    def _():
        xb_ref[...] = x_ref[...].astype(BF16)
        acc_ref[...] = jnp.zeros_like(acc_ref)

    def chunk(rows):
        xb = xb_ref[rows, :]
        gate = jnp.dot(xb, wg_ref[...], preferred_element_type=F32)
        up = jnp.dot(xb, wu_ref[...], preferred_element_type=F32)
        h = (gate * jax.nn.sigmoid(gate) * up).astype(BF16)
        return jnp.dot(h, wd_ref[...], preferred_element_type=F32)

    @pl.when(f < last)
    def _():
        acc_ref[...] += chunk(slice(None))

    @pl.when(f == last)
    def _():
        slab = x_ref.shape[0] // EPILOGUE_SLABS
        for r in range(EPILOGUE_SLABS):
            rows = slice(r * slab, (r + 1) * slab)
            y = ALPHA * x_ref[rows, :] + 0.5 * (acc_ref[rows, :] + chunk(rows))
            o_ref[rows, :] = _layer_norm(y, g_ref[...], b_ref[...])


def _ffn_chunks(w):
    d, f = w.shape
    return w.reshape(d, f // FFN_TF, FFN_TF).transpose(1, 0, 2).astype(BF16)


def _ffn_ln(x, wg, wu, wd, g, b):
    t, d = x.shape
    nf, _, tf = wg.shape
    tm = min(FFN_TM, t)
    return pl.pallas_call(
        _ffn_ln_kernel,
        grid=(t // tm, nf),
        in_specs=[
            pl.BlockSpec((tm, d), lambda i, j: (i, 0)),
            pl.BlockSpec((None, d, tf), lambda i, j: (j, 0, 0)),
            pl.BlockSpec((None, d, tf), lambda i, j: (j, 0, 0)),
            pl.BlockSpec((tf, d), lambda i, j: (j, 0)),
            pl.BlockSpec((1, d), lambda i, j: (0, 0)),
            pl.BlockSpec((1, d), lambda i, j: (0, 0)),
        ],
        out_specs=pl.BlockSpec((tm, d), lambda i, j: (i, 0)),
        out_shape=jax.ShapeDtypeStruct((t, d), F32),
        scratch_shapes=[pltpu.VMEM((tm, d), BF16), pltpu.VMEM((tm, d), F32)],
        compiler_params=_params("parallel", "arbitrary"),
        name="ffn_ln",
    )(x, wg, wu, wd, g, b)


def _in_proj_kernel(x_ref, wk_ref, wqvut_ref, k_ref, qvut_ref):
    xb = x_ref[...].astype(BF16)
    k_ref[...] = jnp.dot(xb, wk_ref[...], preferred_element_type=F32).astype(BF16)
    qvut = lax.dot_general(wqvut_ref[...], xb, (((1,), (1,)), ((), ())),
                           preferred_element_type=F32)
    qvut_ref[...] = qvut.astype(BF16)


def _in_proj(x, w_ku, w_qv_t):
    t, d = x.shape
    tm = min(PROJ_TM, t)
    n_ku, n_qv = w_ku.shape[1], w_qv_t.shape[0]
    return pl.pallas_call(
        _in_proj_kernel,
        grid=(t // tm,),
        in_specs=[
            pl.BlockSpec((tm, d), lambda i: (i, 0)),
            _resident((d, n_ku)),
            _resident((n_qv, d)),
        ],
        out_specs=[
            pl.BlockSpec((tm, n_ku), lambda i: (i, 0)),
            pl.BlockSpec((n_qv, tm), lambda i: (0, i)),
        ],
        out_shape=[jax.ShapeDtypeStruct((t, n_ku), BF16),
                   jax.ShapeDtypeStruct((n_qv, t), BF16)],
        compiler_params=_params("parallel"),
        name="in_proj",
    )(x, w_ku, w_qv_t)


def _diff_attn_kernel(qt_ref, k_ref, vt_ref, lam_ref, g_ref, o_ref,
                      acc_ref, m_ref, l_ref, ls_ref, *, tq, tk, lam_init):
    i = pl.program_id(2)
    dh = DIFF_HEAD_DIM
    ks, qs = ATTN_KS, ATTN_QS
    nkb = tq // tk
    n_strips, n_sub = 2 * tq // qs, tk // ks

    qt = qt_ref[...].astype(F32) * (dh ** -0.5 * LOG2_E)
    row = lax.broadcasted_iota(jnp.int32, qt.shape, 0)
    qz = jnp.concatenate([jnp.where(row < dh, qt, 0.0),
                          jnp.where(row >= dh, qt, 0.0)], axis=1).astype(BF16)

    def init():
        m_ref[...] = jnp.full(m_ref.shape, NEG_BIG, F32)
        l_ref[...] = jnp.zeros_like(l_ref)
        acc_ref[...] = jnp.zeros_like(acc_ref)
        ls_ref[...] = jnp.zeros_like(ls_ref)

    def chunk_mask(shape, k_lo, q_lo):
        kc = (k_lo + lax.broadcasted_iota(jnp.int32, shape, 0)) // CHUNK
        qc = (q_lo + lax.broadcasted_iota(jnp.int32, shape, 1)) // CHUNK
        return kc <= qc

    def exact_step(j, koff):
        k0 = pl.multiple_of(j * tk, tk)
        kb = k_ref[pl.ds(k0, tk), :]
        s = jnp.dot(kb, qz, preferred_element_type=F32)
        if koff is not None:
            keep = chunk_mask((tk, tq), koff, 0)
            s = jnp.where(jnp.concatenate([keep, keep], axis=1), s, NEG_BIG)
        m_old = m_ref[...]
        m_new = jnp.maximum(m_old, jnp.max(s, axis=0, keepdims=True))
        corr = jnp.exp2(m_old - m_new)
        p = jnp.exp2(s - m_new)
        l_ref[...] = corr * l_ref[...] + jnp.sum(p, axis=0, keepdims=True)
        vtb = vt_ref[:, pl.ds(k0, tk)]
        acc_ref[...] = corr * acc_ref[...] + jnp.dot(vtb, p.astype(BF16),
                                                     preferred_element_type=F32)
        m_ref[...] = m_new

    def scores(j, koff, c, r):
        q_lo = (c * qs) % tq
        kb = k_ref[pl.ds(pl.multiple_of(j * tk, tk) + r * ks, ks), :]
        s = jnp.dot(kb, qz[:, c * qs:(c + 1) * qs], preferred_element_type=F32)
        if koff is not None and (koff + r * ks + ks - 1) // CHUNK > q_lo // CHUNK:
            s = jnp.where(chunk_mask(s.shape, koff + r * ks, q_lo), s, NEG_BIG)
        return s

    def prologue():
        init()
        s = jnp.dot(k_ref[0:CHUNK, :], qz, preferred_element_type=F32)
        m_ref[...] = jnp.max(s, axis=0, keepdims=True)

    def fast_steps(blocks):
        def above_diagonal(c, b, r):
            koff = blocks[b][1]
            return koff is not None and (koff + r * ks) // CHUNK > ((c * qs) % tq + qs - 1) // CHUNK

        tiles = [(c, b, r) for c in range(n_strips) for b in range(len(blocks)) for r in range(n_sub)
                 if not above_diagonal(c, b, r)]
        score = lambda c, b, r: scores(blocks[b][0], blocks[b][1], c, r)
        in_flight = [score(*t) for t in tiles[:ATTN_LOOKAHEAD]]
        for n, (c, b, r) in enumerate(tiles):
            lanes = slice(c * qs, (c + 1) * qs)
            s = in_flight.pop(0)
            if n + ATTN_LOOKAHEAD < len(tiles):
                in_flight.append(score(*tiles[n + ATTN_LOOKAHEAD]))
            if n == 0 or tiles[n - 1][0] != c:
                shift = m_ref[:, lanes]
                lsum = ls_ref[:, lanes]
                pv = None
            e = jnp.exp2(s - shift)
            lsum = lsum + jnp.sum(e.reshape(ks // 8, 8, qs), axis=0)
            vtb = vt_ref[:, pl.ds(pl.multiple_of(blocks[b][0] * tk, tk) + r * ks, ks)]
            d = jnp.dot(vtb, e.astype(BF16), preferred_element_type=F32)
            pv = d if pv is None else pv + d
            if n + 1 == len(tiles) or tiles[n + 1][0] != c:
                acc_ref[:, lanes] += pv
                ls_ref[:, lanes] = lsum

    diagonal = [(nkb * i + d, d * tk) for d in range(nkb)]

    def exact_body(j, carry):
        exact_step(j, None)
        return carry

    def fast_body(p, carry):
        fast_steps([(2 * nkb * p + d, None) for d in range(2 * nkb)])
        return carry

    prologue()

    @pl.when(i > 0)
    def _():
        lax.fori_loop(0, i // 2, fast_body, 0)

        @pl.when(i % 2 == 1)
        def _():
            fast_steps([(nkb * (i - 1) + d, None) for d in range(nkb)])

    fast_steps(diagonal)
    in_range = jnp.max(ls_ref[...]) <= 2.0 ** ATTN_SHIFT_SLACK

    @pl.when(jnp.logical_not(in_range))
    def _():
        init()
        lax.fori_loop(0, nkb * i, exact_body, 0)
        for j, koff in diagonal:
            exact_step(j, koff)

    lam_p = lam_ref[...]
    lam = (jnp.exp(jnp.sum(lam_p[0:1] * lam_p[1:2], axis=-1, keepdims=True))
           - jnp.exp(jnp.sum(lam_p[2:3] * lam_p[3:4], axis=-1, keepdims=True))
           + lam_init)
    acc = acc_ref[...]
    l = l_ref[...] + jnp.sum(ls_ref[...], axis=0, keepdims=True)
    ot = acc[:, :tq] / l[:, :tq] - lam * (acc[:, tq:] / l[:, tq:])
    o = ot.T
    o = o * lax.rsqrt(jnp.mean(o * o, axis=-1, keepdims=True) + RMS_EPS)
    o_ref[...] = (o * g_ref[...] * (1.0 - lam_init)).astype(o_ref.dtype)


def _diff_attn(qvt, ku, lam_p, g, lam_init, batch, seq):
    tk = min(ATTN_TK, seq)
    tq = min(ATTN_TQ, seq)
    assert tq % tk == 0 and tk % CHUNK == 0 and seq % tq == 0
    nq = seq // tq
    hh = DIFF_HEADS
    dv = DIFF_V_DIM
    kern = functools.partial(_diff_attn_kernel, tq=tq, tk=tk, lam_init=lam_init)
    return pl.pallas_call(
        kern,
        grid=(batch, hh, nq),
        in_specs=[
            pl.BlockSpec((dv, tq), lambda b, h, i: (h, b * nq + i)),
            pl.BlockSpec((seq, dv), lambda b, h, i: (b, h)),
            pl.BlockSpec((dv, seq), lambda b, h, i: (hh + h, b)),
            pl.BlockSpec((4, DIFF_HEAD_DIM), lambda b, h, i: (0, 0)),
            pl.BlockSpec((1, dv), lambda b, h, i: (0, 0)),
        ],
        out_specs=pl.BlockSpec((tq, dv), lambda b, h, i: (b * nq + i, h)),
        out_shape=jax.ShapeDtypeStruct((batch * seq, hh * dv), BF16),
        scratch_shapes=[pltpu.VMEM((dv, 2 * tq), F32),
                        pltpu.VMEM((1, 2 * tq), F32),
                        pltpu.VMEM((1, 2 * tq), F32),
                        pltpu.VMEM((8, 2 * tq), F32)],
        compiler_params=_params("parallel", "parallel", "arbitrary"),
        name="diff_attn",
    )(qvt, ku, qvt, lam_p, g)


def _s5_kernel(ut_ref, taps_ref, gt_ref, ct_ref, apow_ref, y_ref, mt_ref, *, batch):
    L, P, N = S5_CHUNK, SSM_GROUP, SSM_STATE
    ut = ut_ref[0]
    cols = ut.shape[1]
    per_b = cols // batch

    keep = (lax.broadcasted_iota(jnp.int32, (L, P * L), 1) % L
            <= lax.broadcasted_iota(jnp.int32, (L, P * L), 0))
    for p in range(P):
        base = jnp.broadcast_to(taps_ref[0, p:p + 1, :], (L, P * L))
        rows = pltpu.roll(base, P * L - (L - 1), 1, stride=1, stride_axis=0)
        mt_ref[p * L:(p + 1) * L, :] = jnp.where(keep, rows, 0.0).astype(BF16)

    x = jnp.dot(gt_ref[0], ut, preferred_element_type=F32)
    x_re, x_im = x[:N], x[N:]
    chunk = lax.broadcasted_iota(jnp.int32, (N, cols), 1) % per_b
    for k in range(apow_ref.shape[1]):
        sh = 1 << k
        w_re, w_im = apow_ref[0, k, :, 0:1], apow_ref[0, k, :, 1:2]
        r_re = jnp.where(chunk >= sh, pltpu.roll(x_re, sh, 1), 0.0)
        r_im = jnp.where(chunk >= sh, pltpu.roll(x_im, sh, 1), 0.0)
        x_re, x_im = x_re + (w_re * r_re - w_im * r_im), x_im + (w_re * r_im + w_im * r_re)
    st = jnp.concatenate([jnp.where(chunk >= 1, pltpu.roll(x_re, 1, 1), 0.0),
                          jnp.where(chunk >= 1, pltpu.roll(x_im, 1, 1), 0.0)], axis=0)

    hi = st.astype(BF16)
    lo = (st - hi.astype(F32)).astype(BF16)
    ct = ct_ref[0]
    y = (jnp.dot(mt_ref[...], ut, preferred_element_type=F32)
         + jnp.dot(ct, hi, preferred_element_type=F32)
         + jnp.dot(ct, lo, preferred_element_type=F32))
    y_ref[0] = jax.nn.gelu(y).astype(y_ref.dtype)


def _s5_conv(ut_g, taps, g_t, c_t, a_pow, layer, batch):
    groups, lp, cols = ut_g.shape
    n = SSM_STATE
    kern = functools.partial(_s5_kernel, batch=batch)
    per_group = lambda *shape: pl.BlockSpec((1,) + shape, lambda g: (g,) + (0,) * len(shape))
    of_layer = lambda *shape: pl.BlockSpec((None, 1) + shape, lambda g: (layer, g) + (0,) * len(shape))
    return pl.pallas_call(
        kern,
        grid=(groups,),
        in_specs=[per_group(lp, cols), of_layer(SSM_GROUP, lp), of_layer(2 * n, lp),
                  of_layer(lp, 2 * n), of_layer(*a_pow.shape[2:])],
        out_specs=per_group(lp, cols),
        out_shape=jax.ShapeDtypeStruct((groups, lp, cols), BF16),
        scratch_shapes=[pltpu.VMEM((lp, lp), BF16)],
        compiler_params=_params("parallel"),
        name="s5_conv",
    )(ut_g, taps, g_t, c_t, a_pow)


def _s5_operators(lam_re, lam_im, log_step, b_re, b_im, c_re, c_im, d_skip, n_chunks):
    hp = lax.Precision.HIGHEST
    L, P, G, N = S5_CHUNK, SSM_GROUP, SSM_GROUPS, SSM_STATE
    lr = jnp.minimum(lam_re.astype(F32), -1e-4)
    li = lam_im.astype(F32)
    step = jnp.exp(log_step.astype(F32))[:, None]
    dr, di = lr * step, li * step
    tau = jnp.arange(L + 1, dtype=F32)
    mag = jnp.exp(dr[..., None] * tau)
    pw_re, pw_im = mag * jnp.cos(di[..., None] * tau), mag * jnp.sin(di[..., None] * tau)
    a_re, a_im = pw_re[..., 1], pw_im[..., 1]
    den = lr * lr + li * li
    f_re = ((a_re - 1.0) * lr + a_im * li) / den
    f_im = (a_im * lr - (a_re - 1.0) * li) / den
    br, bi = b_re.astype(F32), b_im.astype(F32)
    bb_re = f_re[..., None] * br - f_im[..., None] * bi
    bb_im = f_re[..., None] * bi + f_im[..., None] * br
    cr, ci = c_re.astype(F32)[:, :, None, :], c_im.astype(F32)[:, :, None, :]
    bq_re, bq_im = bb_re.transpose(0, 2, 1)[:, None], bb_im.transpose(0, 2, 1)[:, None]
    taps = (jnp.einsum('gpqn,gnt->gpqt', cr * bq_re - ci * bq_im, pw_re[..., :L], precision=hp)
            - jnp.einsum('gpqn,gnt->gpqt', cr * bq_im + ci * bq_re, pw_im[..., :L], precision=hp))
    skip = d_skip.astype(F32).reshape(G, P, 1) * jnp.eye(P, dtype=F32)
    taps = taps.at[..., 0].add(skip)
    taps = taps[..., ::-1].reshape(G, P, P * L)
    rev_re, rev_im = pw_re[:, :, None, L - 1::-1], pw_im[:, :, None, L - 1::-1]
    g_t = jnp.concatenate([rev_re * bb_re[..., None] - rev_im * bb_im[..., None],
                           rev_re * bb_im[..., None] + rev_im * bb_re[..., None]],
                          axis=1).reshape(G, 2 * N, P * L)
    nx_re = pw_re[..., 1:].transpose(0, 2, 1)[:, None]
    nx_im = pw_im[..., 1:].transpose(0, 2, 1)[:, None]
    c_t = jnp.concatenate([cr * nx_re - ci * nx_im, -(cr * nx_im + ci * nx_re)],
                          axis=-1).reshape(G, P * L, 2 * N)
    n_steps = max(1, (n_chunks - 1).bit_length())
    span = (L * 2.0 ** jnp.arange(n_steps, dtype=F32))[None, :, None]
    mag = jnp.exp(span * dr[:, None, :])
    a_pow = jnp.stack([mag * jnp.cos(span * di[:, None, :]), mag * jnp.sin(span * di[:, None, :])], axis=-1)
    return taps, g_t.astype(BF16), c_t.astype(BF16), a_pow


def _mix_out_kernel(x_ref, attn_ref, y_ref, gw_ref, gb_ref, wo_ref, g_ref, b_ref, o_ref):
    aw = attn_ref.shape[1]
    slab = x_ref.shape[0] // EPILOGUE_SLABS
    for r in range(EPILOGUE_SLABS):
        rows = slice(r * slab, (r + 1) * slab)
        yb = y_ref[rows, :]
        z = jnp.dot(yb, gw_ref[...], preferred_element_type=F32) + gb_ref[...]
        y = yb.astype(F32) * jax.nn.sigmoid(z)
        mixed = (jnp.dot(attn_ref[rows, :], wo_ref[:aw, :], preferred_element_type=F32)
                 + jnp.dot(y.astype(BF16), wo_ref[aw:, :], preferred_element_type=F32))
        o_ref[rows, :] = _layer_norm(ALPHA * x_ref[rows, :] + mixed, g_ref[...], b_ref[...])


def _mix_out(x, attn, y_ssm, glu_w, glu_b, w_out, g, b):
    t, d = x.shape
    tm = min(PROJ_TM, t)
    aw, sw = attn.shape[1], y_ssm.shape[1]
    row = lambda i: (i, 0)
    return pl.pallas_call(
        _mix_out_kernel,
        grid=(t // tm,),
        in_specs=[
            pl.BlockSpec((tm, d), row),
            pl.BlockSpec((tm, aw), row),
            pl.BlockSpec((tm, sw), row),
            _resident((sw, sw)),
            _resident((1, sw)),
            _resident((d, d)),
            _resident((1, d)),
            _resident((1, d)),
        ],
        out_specs=pl.BlockSpec((tm, d), row),
        out_shape=jax.ShapeDtypeStruct((t, d), F32),
        compiler_params=_params("parallel"),
        name="mix_out",
    )(x, attn, y_ssm, glu_w, glu_b, w_out, g, b)


def _mem_fold_kernel(m_ref, wq_ref, wk_ref, wv_ref, wo_ref, qk_ref, vo_ref):
    mb = m_ref[0].astype(BF16)
    k = jnp.dot(mb, wk_ref[...], preferred_element_type=F32).astype(BF16)
    v = jnp.dot(mb, wv_ref[...], preferred_element_type=F32).astype(BF16)
    qk = lax.dot_general(wq_ref[...], k, (((1,), (1,)), ((), ())),
                         preferred_element_type=F32)
    qk_ref[0] = (qk * (XATTN_HEAD_DIM ** -0.5)).astype(BF16)
    vo_ref[0] = jnp.dot(v, wo_ref[...], preferred_element_type=F32).astype(BF16)


def _mem_fold(mem, wq, wk, wv, wo):
    batch, n_mem, d = mem.shape
    hd = XATTN_HEAD_DIM
    col = lambda b, h: (0, h)
    return pl.pallas_call(
        _mem_fold_kernel,
        grid=(batch, XATTN_HEADS),
        in_specs=[pl.BlockSpec((1, n_mem, d), lambda b, h: (b, 0, 0)),
                  pl.BlockSpec((d, hd), col),
                  pl.BlockSpec((d, hd), col),
                  pl.BlockSpec((d, hd), col),
                  pl.BlockSpec((hd, d), lambda b, h: (h, 0))],
        out_specs=[pl.BlockSpec((1, d, n_mem), lambda b, h: (b, 0, h)),
                   pl.BlockSpec((1, n_mem, d), lambda b, h: (b, h, 0))],
        out_shape=[jax.ShapeDtypeStruct((batch, d, XATTN_HEADS * n_mem), BF16),
                   jax.ShapeDtypeStruct((batch, XATTN_HEADS * n_mem, d), BF16)],
        compiler_params=_params("parallel", "parallel"),
        name="mem_fold",
    )(mem, wq, wk, wv, wo)


def _xattn_ln_kernel(x_ref, qk_ref, vo_ref, g_ref, b_ref, o_ref, *, n_mem):
    x = x_ref[...]
    s = jnp.dot(x.astype(BF16), qk_ref[0], preferred_element_type=F32)
    probs = []
    for h in range(XATTN_HEADS):
        sh = s[:, h * n_mem:(h + 1) * n_mem]
        e = jnp.exp(sh - jnp.max(sh, axis=-1, keepdims=True))
        probs.append((e / jnp.sum(e, axis=-1, keepdims=True)).astype(BF16))
    out = jnp.dot(jnp.concatenate(probs, axis=1), vo_ref[0], preferred_element_type=F32)
    o_ref[...] = _layer_norm(ALPHA * x + out, g_ref[...], b_ref[...])


def _xattn_ln(x, qk, vo, g, b, seq):
    t, d = x.shape
    tm = min(PROJ_TM, seq)
    per_b = seq // tm
    hm = qk.shape[2]
    kern = functools.partial(_xattn_ln_kernel, n_mem=hm // XATTN_HEADS)
    return pl.pallas_call(
        kern,
        grid=(t // tm,),
        in_specs=[
            pl.BlockSpec((tm, d), lambda i: (i, 0)),
            pl.BlockSpec((1, d, hm), lambda i: (i // per_b, 0, 0)),
            pl.BlockSpec((1, hm, d), lambda i: (i // per_b, 0, 0)),
            _resident((1, d)),
            _resident((1, d)),
        ],
        out_specs=pl.BlockSpec((tm, d), lambda i: (i, 0)),
        out_shape=jax.ShapeDtypeStruct((t, d), F32),
        compiler_params=_params("parallel"),
        name="xattn_ln",
    )(x, qk, vo, g, b)


def kernel(x, mem, ffn1_w_gate, ffn1_w_up, ffn1_w_down, ln1_g, ln1_b, w_in, lambda_q1, lambda_k1, lambda_q2, lambda_k2, diff_norm_g, ssm_lambda_re, ssm_lambda_im, ssm_log_step, ssm_b_re, ssm_b_im, ssm_c_re, ssm_c_im, ssm_d, ssm_glu_w, ssm_glu_b, w_out, ln2_g, ln2_b, xattn_w_q, xattn_w_k, xattn_w_v, xattn_w_o, ln3_g, ln3_b, ffn2_w_gate, ffn2_w_up, ffn2_w_down, ln4_g, ln4_b):
    batch, seq, d = x.shape
    t = batch * seq
    aw = ATTN_WIDTH
    L, P, G = S5_CHUNK, SSM_GROUP, SSM_GROUPS
    n_chunks = seq // L
    xf = x.reshape(t, d)
    row = lambda a: a.reshape(1, -1).astype(F32)
    s5_ops = jax.vmap(functools.partial(_s5_operators, n_chunks=n_chunks))(
        ssm_lambda_re, ssm_lambda_im, ssm_log_step, ssm_b_re, ssm_b_im, ssm_c_re, ssm_c_im, ssm_d)

    for l in range(DEPTH):
        lam_init = 0.8 - 0.6 * math.exp(-0.3 * l)
        xf = _ffn_ln(xf, _ffn_chunks(ffn1_w_gate[l]), _ffn_chunks(ffn1_w_up[l]),
                     ffn1_w_down[l].astype(BF16), row(ln1_g[l]), row(ln1_b[l]))

        w = w_in[l]
        w_k = w[:, aw:2 * aw].astype(BF16)
        w_qvu_t = jnp.concatenate([w[:, :aw], w[:, 2 * aw:]], axis=1).T.astype(BF16)
        k_nat, qvut = _in_proj(xf, w_k, w_qvu_t)

        lam_p = jnp.stack([lambda_q1[l], lambda_k1[l], lambda_q2[l], lambda_k2[l]]).astype(F32)
        attn = _diff_attn(qvut, k_nat, lam_p, row(diff_norm_g[l]), lam_init, batch, seq)

        ut_g = (qvut[2 * aw:].reshape(G * P, batch * n_chunks, L)
                .swapaxes(1, 2).reshape(G, P * L, batch * n_chunks))
        yt_g = _s5_conv(ut_g, *s5_ops, l, batch)
        y_ssm = yt_g.reshape(G * P, L, batch * n_chunks).transpose(2, 1, 0).reshape(t, G * P)

        xf = _mix_out(xf, attn, y_ssm, ssm_glu_w[l].astype(BF16),
                      row(ssm_glu_b[l]), w_out[l].astype(BF16), row(ln2_g[l]), row(ln2_b[l]))

        qk, vo = _mem_fold(mem, xattn_w_q[l].astype(BF16), xattn_w_k[l].astype(BF16),
                           xattn_w_v[l].astype(BF16), xattn_w_o[l].astype(BF16))
        xf = _xattn_ln(xf, qk, vo, row(ln3_g[l]), row(ln3_b[l]), seq)

        xf = _ffn_ln(xf, _ffn_chunks(ffn2_w_gate[l]), _ffn_chunks(ffn2_w_up[l]),
                     ffn2_w_down[l].astype(BF16), row(ln4_g[l]), row(ln4_b[l]))
    return xf.reshape(batch, seq, d)
```

```python
import functools
import math

import jax
import jax.numpy as jnp
from jax import lax
from jax.experimental import pallas as pl
from jax.experimental.pallas import tpu as pltpu

F32 = jnp.float32
BF16 = jnp.bfloat16

D_MODEL = 2048
DEPTH = 4
CHUNK = 64
ATTN_WIDTH = D_MODEL // 2
SSM_WIDTH = D_MODEL - ATTN_WIDTH
DIFF_HEAD_DIM = 64
DIFF_V_DIM = 2 * DIFF_HEAD_DIM
DIFF_HEADS = ATTN_WIDTH // DIFF_V_DIM
SSM_GROUP = 16
SSM_GROUPS = SSM_WIDTH // SSM_GROUP
SSM_STATE = 64
XATTN_HEADS = 4
XATTN_HEAD_DIM = D_MODEL // XATTN_HEADS
ALPHA = (2 * DEPTH) ** 0.25
LN_EPS = 1e-5
RMS_EPS = 1e-5
NEG_BIG = -1e30
LOG2_E = 1.4426950408889634

V7X_VMEM_LIMIT_BYTES = 56 * 1024 * 1024

FFN_TM = 512
FFN_TF = 512
PROJ_TM = 512
EPILOGUE_SLABS = 2
ATTN_TQ = 1024
ATTN_TK = 512
ATTN_KS = 256
ATTN_QS = 512
ATTN_LOOKAHEAD = 2
ATTN_SHIFT_SLACK = 64.0
S5_CHUNK = 64


def _params(*sem):
    return pltpu.CompilerParams(dimension_semantics=sem,
                                vmem_limit_bytes=V7X_VMEM_LIMIT_BYTES)


def _resident(shape):
    return pl.BlockSpec(shape, lambda *_: (0,) * len(shape), pipeline_mode=pl.Buffered(1))


def _layer_norm(y, g, b):
    mu = jnp.mean(y, axis=-1, keepdims=True)
    yc = y - mu
    var = jnp.mean(yc * yc, axis=-1, keepdims=True)
    return yc * lax.rsqrt(var + LN_EPS) * g + b


def _ffn_ln_kernel(x_ref, wg_ref, wu_ref, wd_ref, g_ref, b_ref, o_ref, xb_ref, acc_ref):
    f = pl.program_id(1)

    @pl.when(f == 0)
    def _():
        xb_ref[...] = x_ref[...].astype(BF16)
        acc_ref[...] = jnp.zeros_like(acc_ref)

    xb = xb_ref[...]
    gate = jnp.dot(xb, wg_ref[...], preferred_element_type=F32)
    up = jnp.dot(xb, wu_ref[...], preferred_element_type=F32)
    h = (gate * jax.nn.sigmoid(gate) * up).astype(BF16)
    acc_ref[...] += jnp.dot(h, wd_ref[...], preferred_element_type=F32)

    @pl.when(f == pl.num_programs(1) - 1)
    def _():
        y = ALPHA * x_ref[...] + 0.5 * acc_ref[...]
        o_ref[...] = _layer_norm(y, g_ref[...], b_ref[...])


def _ffn_ln(x, wg, wu, wd, g, b):
    t, d = x.shape
    f = wg.shape[1]
    tm, tf = min(FFN_TM, t), FFN_TF
    return pl.pallas_call(
        _ffn_ln_kernel,
        grid=(t // tm, f // tf),
        in_specs=[
            pl.BlockSpec((tm, d), lambda i, j: (i, 0)),
            pl.BlockSpec((d, tf), lambda i, j: (0, j)),
            pl.BlockSpec((d, tf), lambda i, j: (0, j)),
            pl.BlockSpec((tf, d), lambda i, j: (j, 0)),
            pl.BlockSpec((1, d), lambda i, j: (0, 0)),
            pl.BlockSpec((1, d), lambda i, j: (0, 0)),
        ],
        out_specs=pl.BlockSpec((tm, d), lambda i, j: (i, 0)),
        out_shape=jax.ShapeDtypeStruct((t, d), F32),
        scratch_shapes=[pltpu.VMEM((tm, d), BF16), pltpu.VMEM((tm, d), F32)],
        compiler_params=_params("parallel", "arbitrary"),
        name="ffn_ln",
    )(x, wg, wu, wd, g, b)


def _in_proj_kernel(x_ref, wk_ref, wqt_ref, wvut_ref, k_ref, qvut_ref):
    xb = x_ref[...].astype(BF16)
    k_ref[...] = jnp.dot(xb, wk_ref[...], preferred_element_type=F32).astype(BF16)
    nt = (((1,), (1,)), ((), ()))
    nq = wqt_ref.shape[0]
    qvut_ref[:nq, :] = lax.dot_general(wqt_ref[...], xb, nt, preferred_element_type=F32).astype(BF16)
    qvut_ref[nq:, :] = lax.dot_general(wvut_ref[...], xb, nt, preferred_element_type=F32).astype(BF16)


def _in_proj(x, w, w_t):
    t, d = x.shape
    tm = min(PROJ_TM, t)
    aw = ATTN_WIDTH
    n_t = w.shape[1] - aw
    once = pl.Buffered(1)
    return pl.pallas_call(
        _in_proj_kernel,
        grid=(t // tm,),
        in_specs=[
            pl.BlockSpec((tm, d), lambda i: (i, 0)),
            pl.BlockSpec((d, aw), lambda i: (0, 1), pipeline_mode=once),
            pl.BlockSpec((aw, d), lambda i: (0, 0), pipeline_mode=once),
            pl.BlockSpec((2 * aw, d), lambda i: (1, 0), pipeline_mode=once),
        ],
        out_specs=[
            pl.BlockSpec((tm, aw), lambda i: (i, 0)),
            pl.BlockSpec((n_t, tm), lambda i: (0, i)),
        ],
        out_shape=[jax.ShapeDtypeStruct((t, aw), BF16),
                   jax.ShapeDtypeStruct((n_t, t), BF16)],
        compiler_params=_params("parallel"),
        name="in_proj",
    )(x, w, w_t, w_t)


def _diff_attn_kernel(qt_ref, k_ref, vt_ref, lam_ref, g_ref, o_ref,
                      acc_ref, m_ref, l_ref, ls_ref, *, tq, tk, lam_init):
    i = pl.program_id(2)
    dh = DIFF_HEAD_DIM
    ks, qs = ATTN_KS, ATTN_QS
    nkb = tq // tk
    n_strips, n_sub = 2 * tq // qs, tk // ks

    qt = qt_ref[...].astype(F32) * (dh ** -0.5 * LOG2_E)
    row = lax.broadcasted_iota(jnp.int32, qt.shape, 0)
    qz = jnp.concatenate([jnp.where(row < dh, qt, 0.0),
                          jnp.where(row >= dh, qt, 0.0)], axis=1).astype(BF16)

    def init():
        m_ref[...] = jnp.full(m_ref.shape, NEG_BIG, F32)
        l_ref[...] = jnp.zeros_like(l_ref)
        acc_ref[...] = jnp.zeros_like(acc_ref)
        ls_ref[...] = jnp.zeros_like(ls_ref)

    def chunk_mask(shape, k_lo, q_lo):
        kc = (k_lo + lax.broadcasted_iota(jnp.int32, shape, 0)) // CHUNK
        qc = (q_lo + lax.broadcasted_iota(jnp.int32, shape, 1)) // CHUNK
        return kc <= qc

    def exact_step(j, koff):
        k0 = pl.multiple_of(j * tk, tk)
        kb = k_ref[pl.ds(k0, tk), :]
        s = jnp.dot(kb, qz, preferred_element_type=F32)
        if koff is not None:
            keep = chunk_mask((tk, tq), koff, 0)
            s = jnp.where(jnp.concatenate([keep, keep], axis=1), s, NEG_BIG)
        m_old = m_ref[...]
        m_new = jnp.maximum(m_old, jnp.max(s, axis=0, keepdims=True))
        corr = jnp.exp2(m_old - m_new)
        p = jnp.exp2(s - m_new)
        l_ref[...] = corr * l_ref[...] + jnp.sum(p, axis=0, keepdims=True)
        vtb = vt_ref[:, pl.ds(k0, tk)]
        acc_ref[...] = corr * acc_ref[...] + jnp.dot(vtb, p.astype(BF16),
                                                     preferred_element_type=F32)
        m_ref[...] = m_new

    def scores(j, koff, c, r):
        q_lo = (c * qs) % tq
        kb = k_ref[pl.ds(pl.multiple_of(j * tk, tk) + r * ks, ks), :]
        s = jnp.dot(kb, qz[:, c * qs:(c + 1) * qs], preferred_element_type=F32)
        if koff is not None and (koff + r * ks + ks - 1) // CHUNK > q_lo // CHUNK:
            s = jnp.where(chunk_mask(s.shape, koff + r * ks, q_lo), s, NEG_BIG)
        return s

    def prologue():
        init()
        s = jnp.dot(k_ref[0:CHUNK, :], qz, preferred_element_type=F32)
        m_ref[...] = jnp.max(s, axis=0, keepdims=True)

    def fast_steps(blocks):
        def above_diagonal(c, b, r):
            koff = blocks[b][1]
            return koff is not None and (koff + r * ks) // CHUNK > ((c * qs) % tq + qs - 1) // CHUNK

        tiles = [(c, b, r) for c in range(n_strips) for b in range(len(blocks)) for r in range(n_sub)
                 if not above_diagonal(c, b, r)]
        score = lambda c, b, r: scores(blocks[b][0], blocks[b][1], c, r)
        in_flight = [score(*t) for t in tiles[:ATTN_LOOKAHEAD]]
        for n, (c, b, r) in enumerate(tiles):
            lanes = slice(c * qs, (c + 1) * qs)
            s = in_flight.pop(0)
            if n + ATTN_LOOKAHEAD < len(tiles):
                in_flight.append(score(*tiles[n + ATTN_LOOKAHEAD]))
            if n == 0 or tiles[n - 1][0] != c:
                shift = m_ref[:, lanes]
                lsum = ls_ref[:, lanes]
                pv = None
            e = jnp.exp2(s - shift)
            lsum = lsum + jnp.sum(e.reshape(ks // 8, 8, qs), axis=0)
            vtb = vt_ref[:, pl.ds(pl.multiple_of(blocks[b][0] * tk, tk) + r * ks, ks)]
            d = jnp.dot(vtb, e.astype(BF16), preferred_element_type=F32)
            pv = d if pv is None else pv + d
            if n + 1 == len(tiles) or tiles[n + 1][0] != c:
                acc_ref[:, lanes] += pv
                ls_ref[:, lanes] = lsum

    diagonal = [(nkb * i + d, d * tk) for d in range(nkb)]

    def exact_body(j, carry):
        exact_step(j, None)
        return carry

    def fast_body(p, carry):
        fast_steps([(2 * nkb * p + d, None) for d in range(2 * nkb)])
        return carry

    prologue()

    @pl.when(i > 0)
    def _():
        lax.fori_loop(0, i // 2, fast_body, 0)

        @pl.when(i % 2 == 1)
        def _():
            fast_steps([(nkb * (i - 1) + d, None) for d in range(nkb)])

    fast_steps(diagonal)
    in_range = jnp.max(ls_ref[...]) <= 2.0 ** ATTN_SHIFT_SLACK

    @pl.when(jnp.logical_not(in_range))
    def _():
        init()
        lax.fori_loop(0, nkb * i, exact_body, 0)
        for j, koff in diagonal:
            exact_step(j, koff)

    lam_p = lam_ref[...]
    lam = (jnp.exp(jnp.sum(lam_p[0:1] * lam_p[1:2], axis=-1, keepdims=True))
           - jnp.exp(jnp.sum(lam_p[2:3] * lam_p[3:4], axis=-1, keepdims=True))
           + lam_init)
    acc = acc_ref[...]
    l = l_ref[...] + jnp.sum(ls_ref[...], axis=0, keepdims=True)
    ot = acc[:, :tq] / l[:, :tq] - lam * (acc[:, tq:] / l[:, tq:])
    o = ot.T
    o = o * lax.rsqrt(jnp.mean(o * o, axis=-1, keepdims=True) + RMS_EPS)
    o_ref[...] = (o * g_ref[...] * (1.0 - lam_init)).astype(o_ref.dtype)


def _diff_attn(qvt, ku, lam_p, g, lam_init, batch, seq):
    tk = min(ATTN_TK, seq)
    tq = min(ATTN_TQ, seq)
    assert tq % tk == 0 and tk % CHUNK == 0 and seq % tq == 0
    nq = seq // tq
    hh = DIFF_HEADS
    dv = DIFF_V_DIM
    kern = functools.partial(_diff_attn_kernel, tq=tq, tk=tk, lam_init=lam_init)
    return pl.pallas_call(
        kern,
        grid=(batch, hh, nq),
        in_specs=[
            pl.BlockSpec((dv, tq), lambda b, h, i: (h, b * nq + i)),
            pl.BlockSpec((seq, dv), lambda b, h, i: (b, h)),
            pl.BlockSpec((dv, seq), lambda b, h, i: (hh + h, b)),
            pl.BlockSpec((4, DIFF_HEAD_DIM), lambda b, h, i: (0, 0)),
            pl.BlockSpec((1, dv), lambda b, h, i: (0, 0)),
        ],
        out_specs=pl.BlockSpec((tq, dv), lambda b, h, i: (b * nq + i, h)),
        out_shape=jax.ShapeDtypeStruct((batch * seq, hh * dv), BF16),
        scratch_shapes=[pltpu.VMEM((dv, 2 * tq), F32),
                        pltpu.VMEM((1, 2 * tq), F32),
                        pltpu.VMEM((1, 2 * tq), F32),
                        pltpu.VMEM((8, 2 * tq), F32)],
        compiler_params=_params("parallel", "parallel", "arbitrary"),
        name="diff_attn",
    )(qvt, ku, qvt, lam_p, g)


def _s5_kernel(ut_ref, taps_ref, gt_ref, ct_ref, apow_ref, y_ref, mt_ref, *, batch):
    L, P, N = S5_CHUNK, SSM_GROUP, SSM_STATE
    ut = ut_ref[0]
    cols = ut.shape[1]
    per_b = cols // batch

    keep = (lax.broadcasted_iota(jnp.int32, (L, P * L), 1) % L
            <= lax.broadcasted_iota(jnp.int32, (L, P * L), 0))
    for p in range(P):
        base = jnp.broadcast_to(taps_ref[0, p:p + 1, :], (L, P * L))
        rows = pltpu.roll(base, P * L - (L - 1), 1, stride=1, stride_axis=0)
        mt_ref[p * L:(p + 1) * L, :] = jnp.where(keep, rows, 0.0).astype(BF16)

    x = jnp.dot(gt_ref[0], ut, preferred_element_type=F32)
    x_re, x_im = x[:N], x[N:]
    chunk = lax.broadcasted_iota(jnp.int32, (N, cols), 1) % per_b
    for k in range(apow_ref.shape[1]):
        sh = 1 << k
        w_re, w_im = apow_ref[0, k, :, 0:1], apow_ref[0, k, :, 1:2]
        r_re = jnp.where(chunk >= sh, pltpu.roll(x_re, sh, 1), 0.0)
        r_im = jnp.where(chunk >= sh, pltpu.roll(x_im, sh, 1), 0.0)
        x_re, x_im = x_re + (w_re * r_re - w_im * r_im), x_im + (w_re * r_im + w_im * r_re)
    st = jnp.concatenate([jnp.where(chunk >= 1, pltpu.roll(x_re, 1, 1), 0.0),
                          jnp.where(chunk >= 1, pltpu.roll(x_im, 1, 1), 0.0)], axis=0)

    hi = st.astype(BF16)
    lo = (st - hi.astype(F32)).astype(BF16)
    ct = ct_ref[0]
    y = (jnp.dot(mt_ref[...], ut, preferred_element_type=F32)
         + jnp.dot(ct, hi, preferred_element_type=F32)
         + jnp.dot(ct, lo, preferred_element_type=F32))
    y_ref[0] = jax.nn.gelu(y).astype(y_ref.dtype)


def _s5_conv(ut_g, taps, g_t, c_t, a_pow, layer, batch):
    groups, lp, cols = ut_g.shape
    n = SSM_STATE
    kern = functools.partial(_s5_kernel, batch=batch)
    per_group = lambda *shape: pl.BlockSpec((1,) + shape, lambda g: (g,) + (0,) * len(shape))
    of_layer = lambda *shape: pl.BlockSpec((None, 1) + shape, lambda g: (layer, g) + (0,) * len(shape))
    return pl.pallas_call(
        kern,
        grid=(groups,),
        in_specs=[per_group(lp, cols), of_layer(SSM_GROUP, lp), of_layer(2 * n, lp),
                  of_layer(lp, 2 * n), of_layer(*a_pow.shape[2:])],
        out_specs=per_group(lp, cols),
        out_shape=jax.ShapeDtypeStruct((groups, lp, cols), BF16),
        scratch_shapes=[pltpu.VMEM((lp, lp), BF16)],
        compiler_params=_params("parallel"),
        name="s5_conv",
    )(ut_g, taps, g_t, c_t, a_pow)


def _s5_operators(lam_re, lam_im, log_step, b_re, b_im, c_re, c_im, d_skip, n_chunks):
    hp = lax.Precision.HIGHEST
    L, P, G, N = S5_CHUNK, SSM_GROUP, SSM_GROUPS, SSM_STATE
    lr = jnp.minimum(lam_re.astype(F32), -1e-4)
    li = lam_im.astype(F32)
    step = jnp.exp(log_step.astype(F32))[:, None]
    dr, di = lr * step, li * step
    tau = jnp.arange(L + 1, dtype=F32)
    mag = jnp.exp(dr[..., None] * tau)
    pw_re, pw_im = mag * jnp.cos(di[..., None] * tau), mag * jnp.sin(di[..., None] * tau)
    a_re, a_im = pw_re[..., 1], pw_im[..., 1]
    den = lr * lr + li * li
    f_re = ((a_re - 1.0) * lr + a_im * li) / den
    f_im = (a_im * lr - (a_re - 1.0) * li) / den
    br, bi = b_re.astype(F32), b_im.astype(F32)
    bb_re = f_re[..., None] * br - f_im[..., None] * bi
    bb_im = f_re[..., None] * bi + f_im[..., None] * br
    cr, ci = c_re.astype(F32)[:, :, None, :], c_im.astype(F32)[:, :, None, :]
    bq_re, bq_im = bb_re.transpose(0, 2, 1)[:, None], bb_im.transpose(0, 2, 1)[:, None]
    taps = (jnp.einsum('gpqn,gnt->gpqt', cr * bq_re - ci * bq_im, pw_re[..., :L], precision=hp)
            - jnp.einsum('gpqn,gnt->gpqt', cr * bq_im + ci * bq_re, pw_im[..., :L], precision=hp))
    skip = d_skip.astype(F32).reshape(G, P, 1) * jnp.eye(P, dtype=F32)
    taps = taps.at[..., 0].add(skip)
    taps = taps[..., ::-1].reshape(G, P, P * L)
    rev_re, rev_im = pw_re[:, :, None, L - 1::-1], pw_im[:, :, None, L - 1::-1]
    g_t = jnp.concatenate([rev_re * bb_re[..., None] - rev_im * bb_im[..., None],
                           rev_re * bb_im[..., None] + rev_im * bb_re[..., None]],
                          axis=1).reshape(G, 2 * N, P * L)
    nx_re = pw_re[..., 1:].transpose(0, 2, 1)[:, None]
    nx_im = pw_im[..., 1:].transpose(0, 2, 1)[:, None]
    c_t = jnp.concatenate([cr * nx_re - ci * nx_im, -(cr * nx_im + ci * nx_re)],
                          axis=-1).reshape(G, P * L, 2 * N)
    n_steps = max(1, (n_chunks - 1).bit_length())
    span = (L * 2.0 ** jnp.arange(n_steps, dtype=F32))[None, :, None]
    mag = jnp.exp(span * dr[:, None, :])
    a_pow = jnp.stack([mag * jnp.cos(span * di[:, None, :]), mag * jnp.sin(span * di[:, None, :])], axis=-1)
    return taps, g_t.astype(BF16), c_t.astype(BF16), a_pow


def _mix_out_kernel(x_ref, attn_ref, y_ref, gw_ref, gb_ref, wo_ref, g_ref, b_ref, o_ref):
    aw = attn_ref.shape[1]
    slab = x_ref.shape[0] // EPILOGUE_SLABS
    for r in range(EPILOGUE_SLABS):
        rows = slice(r * slab, (r + 1) * slab)
        yb = y_ref[rows, :]
        z = jnp.dot(yb, gw_ref[...], preferred_element_type=F32) + gb_ref[...]
        y = yb.astype(F32) * jax.nn.sigmoid(z)
        mixed = (jnp.dot(attn_ref[rows, :], wo_ref[:aw, :], preferred_element_type=F32)
                 + jnp.dot(y.astype(BF16), wo_ref[aw:, :], preferred_element_type=F32))
        o_ref[rows, :] = _layer_norm(ALPHA * x_ref[rows, :] + mixed, g_ref[...], b_ref[...])


def _mix_out(x, attn, y_ssm, glu_w, glu_b, w_out, g, b):
    t, d = x.shape
    tm = min(PROJ_TM, t)
    aw, sw = attn.shape[1], y_ssm.shape[1]
    row = lambda i: (i, 0)
    return pl.pallas_call(
        _mix_out_kernel,
        grid=(t // tm,),
        in_specs=[
            pl.BlockSpec((tm, d), row),
            pl.BlockSpec((tm, aw), row),
            pl.BlockSpec((tm, sw), row),
            _resident((sw, sw)),
            _resident((1, sw)),
            _resident((d, d)),
            _resident((1, d)),
            _resident((1, d)),
        ],
        out_specs=pl.BlockSpec((tm, d), row),
        out_shape=jax.ShapeDtypeStruct((t, d), F32),
        compiler_params=_params("parallel"),
        name="mix_out",
    )(x, attn, y_ssm, glu_w, glu_b, w_out, g, b)


def _mem_fold_kernel(m_ref, wq_ref, wk_ref, wv_ref, wo_ref, qk_ref, vo_ref):
    mb = m_ref[0].astype(BF16)
    k = jnp.dot(mb, wk_ref[...], preferred_element_type=F32).astype(BF16)
    v = jnp.dot(mb, wv_ref[...], preferred_element_type=F32).astype(BF16)
    qk = lax.dot_general(wq_ref[...], k, (((1,), (1,)), ((), ())),
                         preferred_element_type=F32)
    qk_ref[0] = (qk * (XATTN_HEAD_DIM ** -0.5)).astype(BF16)
    vo_ref[0] = jnp.dot(v, wo_ref[...], preferred_element_type=F32).astype(BF16)


def _mem_fold(mem, wq, wk, wv, wo):
    batch, n_mem, d = mem.shape
    hd = XATTN_HEAD_DIM
    col = lambda b, h: (0, h)
    return pl.pallas_call(
        _mem_fold_kernel,
        grid=(batch, XATTN_HEADS),
        in_specs=[pl.BlockSpec((1, n_mem, d), lambda b, h: (b, 0, 0)),
                  pl.BlockSpec((d, hd), col),
                  pl.BlockSpec((d, hd), col),
                  pl.BlockSpec((d, hd), col),
                  pl.BlockSpec((hd, d), lambda b, h: (h, 0))],
        out_specs=[pl.BlockSpec((1, d, n_mem), lambda b, h: (b, 0, h)),
                   pl.BlockSpec((1, n_mem, d), lambda b, h: (b, h, 0))],
        out_shape=[jax.ShapeDtypeStruct((batch, d, XATTN_HEADS * n_mem), BF16),
                   jax.ShapeDtypeStruct((batch, XATTN_HEADS * n_mem, d), BF16)],
        compiler_params=_params("parallel", "parallel"),
        name="mem_fold",
    )(mem, wq, wk, wv, wo)


def _xattn_ln_kernel(x_ref, qk_ref, vo_ref, g_ref, b_ref, o_ref, *, n_mem):
    x = x_ref[...]
    s = jnp.dot(x.astype(BF16), qk_ref[0], preferred_element_type=F32)
    probs = []
    for h in range(XATTN_HEADS):
        sh = s[:, h * n_mem:(h + 1) * n_mem]
        e = jnp.exp(sh - jnp.max(sh, axis=-1, keepdims=True))
        probs.append((e / jnp.sum(e, axis=-1, keepdims=True)).astype(BF16))
    out = jnp.dot(jnp.concatenate(probs, axis=1), vo_ref[0], preferred_element_type=F32)
    o_ref[...] = _layer_norm(ALPHA * x + out, g_ref[...], b_ref[...])


def _xattn_ln(x, qk, vo, g, b, seq):
    t, d = x.shape
    tm = min(PROJ_TM, seq)
    per_b = seq // tm
    hm = qk.shape[2]
    kern = functools.partial(_xattn_ln_kernel, n_mem=hm // XATTN_HEADS)
    return pl.pallas_call(
        kern,
        grid=(t // tm,),
        in_specs=[
            pl.BlockSpec((tm, d), lambda i: (i, 0)),
            pl.BlockSpec((1, d, hm), lambda i: (i // per_b, 0, 0)),
            pl.BlockSpec((1, hm, d), lambda i: (i // per_b, 0, 0)),
            _resident((1, d)),
            _resident((1, d)),
        ],
        out_specs=pl.BlockSpec((tm, d), lambda i: (i, 0)),
        out_shape=jax.ShapeDtypeStruct((t, d), F32),
        compiler_params=_params("parallel"),
        name="xattn_ln",
    )(x, qk, vo, g, b)


def kernel(x, mem, ffn1_w_gate, ffn1_w_up, ffn1_w_down, ln1_g, ln1_b, w_in, lambda_q1, lambda_k1, lambda_q2, lambda_k2, diff_norm_g, ssm_lambda_re, ssm_lambda_im, ssm_log_step, ssm_b_re, ssm_b_im, ssm_c_re, ssm_c_im, ssm_d, ssm_glu_w, ssm_glu_b, w_out, ln2_g, ln2_b, xattn_w_q, xattn_w_k, xattn_w_v, xattn_w_o, ln3_g, ln3_b, ffn2_w_gate, ffn2_w_up, ffn2_w_down, ln4_g, ln4_b):
    batch, seq, d = x.shape
    t = batch * seq
    aw = ATTN_WIDTH
    L, P, G = S5_CHUNK, SSM_GROUP, SSM_GROUPS
    n_chunks = seq // L
    xf = x.reshape(t, d)
    row = lambda a: a.reshape(1, -1).astype(F32)
    s5_ops = jax.vmap(functools.partial(_s5_operators, n_chunks=n_chunks))(
        ssm_lambda_re, ssm_lambda_im, ssm_log_step, ssm_b_re, ssm_b_im, ssm_c_re, ssm_c_im, ssm_d)

    for l in range(DEPTH):
        lam_init = 0.8 - 0.6 * math.exp(-0.3 * l)
        xf = _ffn_ln(xf, ffn1_w_gate[l].astype(BF16), ffn1_w_up[l].astype(BF16),
                     ffn1_w_down[l].astype(BF16), row(ln1_g[l]), row(ln1_b[l]))

        k_nat, qvut = _in_proj(xf, w_in[l].astype(BF16), w_in[l].T.astype(BF16))

        lam_p = jnp.stack([lambda_q1[l], lambda_k1[l], lambda_q2[l], lambda_k2[l]]).astype(F32)
        attn = _diff_attn(qvut, k_nat, lam_p, row(diff_norm_g[l]), lam_init, batch, seq)

        ut_g = (qvut[2 * aw:].reshape(G * P, batch * n_chunks, L)
                .swapaxes(1, 2).reshape(G, P * L, batch * n_chunks))
        yt_g = _s5_conv(ut_g, *s5_ops, l, batch)
        y_ssm = yt_g.reshape(G * P, L, batch * n_chunks).transpose(2, 1, 0).reshape(t, G * P)

        xf = _mix_out(xf, attn, y_ssm, ssm_glu_w[l].astype(BF16),
                      row(ssm_glu_b[l]), w_out[l].astype(BF16), row(ln2_g[l]), row(ln2_b[l]))

        qk, vo = _mem_fold(mem, xattn_w_q[l].astype(BF16), xattn_w_k[l].astype(BF16),
                           xattn_w_v[l].astype(BF16), xattn_w_o[l].astype(BF16))
        xf = _xattn_ln(xf, qk, vo, row(ln3_g[l]), row(ln3_b[l]), seq)

        xf = _ffn_ln(xf, ffn2_w_gate[l].astype(BF16), ffn2_w_up[l].astype(BF16),
                     ffn2_w_down[l].astype(BF16), row(ln4_g[l]), row(ln4_b[l]))
    return xf.reshape(batch, seq, d)
```

```python
import functools
import math

import jax
import jax.numpy as jnp
from jax import lax
from jax.experimental import pallas as pl
from jax.experimental.pallas import tpu as pltpu

F32 = jnp.float32
BF16 = jnp.bfloat16

D_MODEL = 2048
DEPTH = 4
CHUNK = 64
ATTN_WIDTH = D_MODEL // 2
SSM_WIDTH = D_MODEL - ATTN_WIDTH
DIFF_HEAD_DIM = 64
DIFF_V_DIM = 2 * DIFF_HEAD_DIM
DIFF_HEADS = ATTN_WIDTH // DIFF_V_DIM
SSM_GROUP = 16
SSM_GROUPS = SSM_WIDTH // SSM_GROUP
SSM_STATE = 64
XATTN_HEADS = 4
XATTN_HEAD_DIM = D_MODEL // XATTN_HEADS
ALPHA = (2 * DEPTH) ** 0.25
LN_EPS = 1e-5
RMS_EPS = 1e-5
NEG_BIG = -1e30
LOG2_E = 1.4426950408889634

V7X_VMEM_LIMIT_BYTES = 56 * 1024 * 1024
F32_SUBLANES = 8

FFN_TM = 512
FFN_TF = 512
PROJ_TM = 512
EPILOGUE_SLABS = 2
ATTN_TQ = 1024
ATTN_TK = 512
ATTN_KS = 256
ATTN_QS = 512
ATTN_LOOKAHEAD = 2
ATTN_SHIFT_SLACK = 64.0
S5_CHUNK = 64


def _params(*sem):
    return pltpu.CompilerParams(dimension_semantics=sem,
                                vmem_limit_bytes=V7X_VMEM_LIMIT_BYTES)


def _resident(shape):
    return pl.BlockSpec(shape, lambda *_: (0,) * len(shape), pipeline_mode=pl.Buffered(1))


def _layer_norm(y, g, b):
    mu = jnp.mean(y, axis=-1, keepdims=True)
    yc = y - mu
    var = jnp.mean(yc * yc, axis=-1, keepdims=True)
    return yc * lax.rsqrt(var + LN_EPS) * g + b


def _ffn_ln_kernel(x_ref, wg_ref, wu_ref, wd_ref, g_ref, b_ref, o_ref, xb_ref, acc_ref):
    f = pl.program_id(1)

    @pl.when(f == 0)
    def _():
        xb_ref[...] = x_ref[...].astype(BF16)
        acc_ref[...] = jnp.zeros_like(acc_ref)

    xb = xb_ref[...]
    gate = jnp.dot(xb, wg_ref[...], preferred_element_type=F32)
    up = jnp.dot(xb, wu_ref[...], preferred_element_type=F32)
    h = (gate * jax.nn.sigmoid(gate) * up).astype(BF16)
    acc_ref[...] += jnp.dot(h, wd_ref[...], preferred_element_type=F32)

    @pl.when(f == pl.num_programs(1) - 1)
    def _():
        y = ALPHA * x_ref[...] + 0.5 * acc_ref[...]
        o_ref[...] = _layer_norm(y, g_ref[...], b_ref[...])


def _ffn_ln(x, wg, wu, wd, g, b):
    t, d = x.shape
    f = wg.shape[1]
    tm, tf = min(FFN_TM, t), FFN_TF
    return pl.pallas_call(
        _ffn_ln_kernel,
        grid=(t // tm, f // tf),
        in_specs=[
            pl.BlockSpec((tm, d), lambda i, j: (i, 0)),
            pl.BlockSpec((d, tf), lambda i, j: (0, j)),
            pl.BlockSpec((d, tf), lambda i, j: (0, j)),
            pl.BlockSpec((tf, d), lambda i, j: (j, 0)),
            pl.BlockSpec((1, d), lambda i, j: (0, 0)),
            pl.BlockSpec((1, d), lambda i, j: (0, 0)),
        ],
        out_specs=pl.BlockSpec((tm, d), lambda i, j: (i, 0)),
        out_shape=jax.ShapeDtypeStruct((t, d), F32),
        scratch_shapes=[pltpu.VMEM((tm, d), BF16), pltpu.VMEM((tm, d), F32)],
        compiler_params=_params("parallel", "arbitrary"),
        name="ffn_ln",
    )(x, wg, wu, wd, g, b)


def _in_proj_kernel(x_ref, wk_ref, wqt_ref, wvut_ref, k_ref, qvut_ref):
    xb = x_ref[...].astype(BF16)
    k_ref[...] = jnp.dot(xb, wk_ref[...], preferred_element_type=F32).astype(BF16)
    nt = (((1,), (1,)), ((), ()))
    nq = wqt_ref.shape[0]
    qvut_ref[:nq, :] = lax.dot_general(wqt_ref[...], xb, nt, preferred_element_type=F32).astype(BF16)
    qvut_ref[nq:, :] = lax.dot_general(wvut_ref[...], xb, nt, preferred_element_type=F32).astype(BF16)


def _in_proj(x, w, w_t):
    t, d = x.shape
    tm = min(PROJ_TM, t)
    aw = ATTN_WIDTH
    n_t = w.shape[1] - aw
    once = pl.Buffered(1)
    return pl.pallas_call(
        _in_proj_kernel,
        grid=(t // tm,),
        in_specs=[
            pl.BlockSpec((tm, d), lambda i: (i, 0)),
            pl.BlockSpec((d, aw), lambda i: (0, 1), pipeline_mode=once),
            pl.BlockSpec((aw, d), lambda i: (0, 0), pipeline_mode=once),
            pl.BlockSpec((2 * aw, d), lambda i: (1, 0), pipeline_mode=once),
        ],
        out_specs=[
            pl.BlockSpec((tm, aw), lambda i: (i, 0)),
            pl.BlockSpec((n_t, tm), lambda i: (0, i)),
        ],
        out_shape=[jax.ShapeDtypeStruct((t, aw), BF16),
                   jax.ShapeDtypeStruct((n_t, t), BF16)],
        compiler_params=_params("parallel"),
        name="in_proj",
    )(x, w, w_t, w_t)


def _diff_attn_kernel(qt_ref, k_ref, vt_ref, lam_ref, g_ref, o_ref,
                      acc_ref, m_ref, l_ref, ls_ref, *, tq, tk, lam_init):
    i = pl.program_id(2)
    dh = DIFF_HEAD_DIM
    ks, qs = ATTN_KS, ATTN_QS
    nkb = tq // tk
    n_strips, n_sub = 2 * tq // qs, tk // ks

    qt = qt_ref[...].astype(F32) * (dh ** -0.5 * LOG2_E)
    row = lax.broadcasted_iota(jnp.int32, qt.shape, 0)
    qz = jnp.concatenate([jnp.where(row < dh, qt, 0.0),
                          jnp.where(row >= dh, qt, 0.0)], axis=1).astype(BF16)

    def init():
        m_ref[...] = jnp.full(m_ref.shape, NEG_BIG, F32)
        l_ref[...] = jnp.zeros_like(l_ref)
        acc_ref[...] = jnp.zeros_like(acc_ref)
        ls_ref[...] = jnp.zeros_like(ls_ref)

    def chunk_mask(shape, k_lo, q_lo):
        kc = (k_lo + lax.broadcasted_iota(jnp.int32, shape, 0)) // CHUNK
        qc = (q_lo + lax.broadcasted_iota(jnp.int32, shape, 1)) // CHUNK
        return kc <= qc

    def exact_step(j, koff):
        k0 = pl.multiple_of(j * tk, tk)
        kb = k_ref[pl.ds(k0, tk), :]
        s = jnp.dot(kb, qz, preferred_element_type=F32)
        if koff is not None:
            keep = chunk_mask((tk, tq), koff, 0)
            s = jnp.where(jnp.concatenate([keep, keep], axis=1), s, NEG_BIG)
        m_old = m_ref[...]
        m_new = jnp.maximum(m_old, jnp.max(s, axis=0, keepdims=True))
        corr = jnp.exp2(m_old - m_new)
        p = jnp.exp2(s - m_new)
        l_ref[...] = corr * l_ref[...] + jnp.sum(p, axis=0, keepdims=True)
        vtb = vt_ref[:, pl.ds(k0, tk)]
        acc_ref[...] = corr * acc_ref[...] + jnp.dot(vtb, p.astype(BF16),
                                                     preferred_element_type=F32)
        m_ref[...] = m_new

    def scores(j, koff, c, r):
        q_lo = (c * qs) % tq
        kb = k_ref[pl.ds(pl.multiple_of(j * tk, tk) + r * ks, ks), :]
        s = jnp.dot(kb, qz[:, c * qs:(c + 1) * qs], preferred_element_type=F32)
        if koff is not None and (koff + r * ks + ks - 1) // CHUNK > q_lo // CHUNK:
            s = jnp.where(chunk_mask(s.shape, koff + r * ks, q_lo), s, NEG_BIG)
        return s

    def prologue():
        init()
        s = jnp.dot(k_ref[0:CHUNK, :], qz, preferred_element_type=F32)
        m_ref[...] = jnp.max(s, axis=0, keepdims=True)

    def fast_steps(blocks):
        def above_diagonal(c, b, r):
            koff = blocks[b][1]
            return koff is not None and (koff + r * ks) // CHUNK > ((c * qs) % tq + qs - 1) // CHUNK

        tiles = [(c, b, r) for c in range(n_strips) for b in range(len(blocks)) for r in range(n_sub)
                 if not above_diagonal(c, b, r)]
        score = lambda c, b, r: scores(blocks[b][0], blocks[b][1], c, r)
        in_flight = [score(*t) for t in tiles[:ATTN_LOOKAHEAD]]
        for n, (c, b, r) in enumerate(tiles):
            lanes = slice(c * qs, (c + 1) * qs)
            s = in_flight.pop(0)
            if n + ATTN_LOOKAHEAD < len(tiles):
                in_flight.append(score(*tiles[n + ATTN_LOOKAHEAD]))
            if n == 0 or tiles[n - 1][0] != c:
                shift = m_ref[:, lanes]
                lsum = ls_ref[:, lanes]
                pv = None
            e = jnp.exp2(s - shift)
            lsum = lsum + jnp.sum(e.reshape(ks // F32_SUBLANES, F32_SUBLANES, qs), axis=0)
            vtb = vt_ref[:, pl.ds(pl.multiple_of(blocks[b][0] * tk, tk) + r * ks, ks)]
            d = jnp.dot(vtb, e.astype(BF16), preferred_element_type=F32)
            pv = d if pv is None else pv + d
            if n + 1 == len(tiles) or tiles[n + 1][0] != c:
                acc_ref[:, lanes] += pv
                ls_ref[:, lanes] = lsum

    diagonal = [(nkb * i + d, d * tk) for d in range(nkb)]

    def exact_body(j, carry):
        exact_step(j, None)
        return carry

    def fast_body(p, carry):
        fast_steps([(2 * nkb * p + d, None) for d in range(2 * nkb)])
        return carry

    prologue()

    @pl.when(i > 0)
    def _():
        lax.fori_loop(0, i // 2, fast_body, 0)

        @pl.when(i % 2 == 1)
        def _():
            fast_steps([(nkb * (i - 1) + d, None) for d in range(nkb)])

    fast_steps(diagonal)
    in_range = jnp.max(ls_ref[...]) <= 2.0 ** ATTN_SHIFT_SLACK

    @pl.when(jnp.logical_not(in_range))
    def _():
        init()
        lax.fori_loop(0, nkb * i, exact_body, 0)
        for j, koff in diagonal:
            exact_step(j, koff)

    lam_p = lam_ref[...]
    lam = (jnp.exp(jnp.sum(lam_p[0:1] * lam_p[1:2], axis=-1, keepdims=True))
           - jnp.exp(jnp.sum(lam_p[2:3] * lam_p[3:4], axis=-1, keepdims=True))
           + lam_init)
    acc = acc_ref[...]
    l = l_ref[...] + jnp.sum(ls_ref[...], axis=0, keepdims=True)
    ot = acc[:, :tq] / l[:, :tq] - lam * (acc[:, tq:] / l[:, tq:])
    ot = ot * lax.rsqrt(jnp.mean(ot * ot, axis=0, keepdims=True) + RMS_EPS)
    o_ref[...] = (ot * g_ref[...] * (1.0 - lam_init)).T.astype(o_ref.dtype)


def _diff_attn(qvt, ku, lam_p, g, lam_init, batch, seq):
    tk = min(ATTN_TK, seq)
    tq = min(ATTN_TQ, seq)
    assert tq % tk == 0 and tk % CHUNK == 0 and seq % tq == 0
    nq = seq // tq
    hh = DIFF_HEADS
    dv = DIFF_V_DIM
    kern = functools.partial(_diff_attn_kernel, tq=tq, tk=tk, lam_init=lam_init)
    return pl.pallas_call(
        kern,
        grid=(batch, hh, nq),
        in_specs=[
            pl.BlockSpec((dv, tq), lambda b, h, i: (h, b * nq + i)),
            pl.BlockSpec((seq, dv), lambda b, h, i: (b, h)),
            pl.BlockSpec((dv, seq), lambda b, h, i: (hh + h, b)),
            pl.BlockSpec((4, DIFF_HEAD_DIM), lambda b, h, i: (0, 0)),
            pl.BlockSpec((dv, 1), lambda b, h, i: (0, 0)),
        ],
        out_specs=pl.BlockSpec((tq, dv), lambda b, h, i: (b * nq + i, h)),
        out_shape=jax.ShapeDtypeStruct((batch * seq, hh * dv), BF16),
        scratch_shapes=[pltpu.VMEM((dv, 2 * tq), F32),
                        pltpu.VMEM((1, 2 * tq), F32),
                        pltpu.VMEM((1, 2 * tq), F32),
                        pltpu.VMEM((F32_SUBLANES, 2 * tq), F32)],
        compiler_params=_params("parallel", "parallel", "arbitrary"),
        name="diff_attn",
    )(qvt, ku, qvt, lam_p, g)


def _s5_kernel(ut_ref, taps_ref, gt_ref, ct_ref, apow_ref, y_ref, mt_ref, *, batch):
    L, P, N = S5_CHUNK, SSM_GROUP, SSM_STATE
    ut = ut_ref[0]
    cols = ut.shape[1]
    per_b = cols // batch

    x = jnp.dot(gt_ref[0], ut, preferred_element_type=F32)
    x_re, x_im = x[:N], x[N:]
    chunk = lax.broadcasted_iota(jnp.int32, (N, cols), 1) % per_b
    for k in range(apow_ref.shape[1]):
        sh = 1 << k
        w_re, w_im = apow_ref[0, k, :, 0:1], apow_ref[0, k, :, 1:2]
        r_re = jnp.where(chunk >= sh, pltpu.roll(x_re, sh, 1), 0.0)
        r_im = jnp.where(chunk >= sh, pltpu.roll(x_im, sh, 1), 0.0)
        x_re, x_im = x_re + (w_re * r_re - w_im * r_im), x_im + (w_re * r_im + w_im * r_re)
    st = jnp.concatenate([jnp.where(chunk >= 1, pltpu.roll(x_re, 1, 1), 0.0),
                          jnp.where(chunk >= 1, pltpu.roll(x_im, 1, 1), 0.0)], axis=0)
    hi = st.astype(BF16)
    lo = (st - hi.astype(F32)).astype(BF16)

    keep = (lax.broadcasted_iota(jnp.int32, (L, P * L), 1) % L
            <= lax.broadcasted_iota(jnp.int32, (L, P * L), 0))
    for p in range(P):
        base = jnp.broadcast_to(taps_ref[0, p:p + 1, :], (L, P * L))
        rows = pltpu.roll(base, P * L - (L - 1), 1, stride=1, stride_axis=0)
        mt_ref[p * L:(p + 1) * L, :] = jnp.where(keep, rows, 0.0).astype(BF16)

    ct = ct_ref[0]
    y = (jnp.dot(mt_ref[...], ut, preferred_element_type=F32)
         + jnp.dot(ct, hi, preferred_element_type=F32)
         + jnp.dot(ct, lo, preferred_element_type=F32))
    y_ref[0] = jax.nn.gelu(y).astype(y_ref.dtype)


def _s5_conv(ut_g, taps, g_t, c_t, a_pow, layer, batch):
    groups, lp, cols = ut_g.shape
    n = SSM_STATE
    kern = functools.partial(_s5_kernel, batch=batch)
    per_group = lambda *shape: pl.BlockSpec((1,) + shape, lambda g: (g,) + (0,) * len(shape))
    of_layer = lambda *shape: pl.BlockSpec((None, 1) + shape, lambda g: (layer, g) + (0,) * len(shape))
    return pl.pallas_call(
        kern,
        grid=(groups,),
        in_specs=[per_group(lp, cols), of_layer(SSM_GROUP, lp), of_layer(2 * n, lp),
                  of_layer(lp, 2 * n), of_layer(*a_pow.shape[2:])],
        out_specs=per_group(lp, cols),
        out_shape=jax.ShapeDtypeStruct((groups, lp, cols), BF16),
        scratch_shapes=[pltpu.VMEM((lp, lp), BF16)],
        compiler_params=_params("parallel"),
        name="s5_conv",
    )(ut_g, taps, g_t, c_t, a_pow)


def _s5_operators(lam_re, lam_im, log_step, b_re, b_im, c_re, c_im, d_skip, n_chunks):
    hp = lax.Precision.HIGHEST
    L, P, G, N = S5_CHUNK, SSM_GROUP, SSM_GROUPS, SSM_STATE
    lr = jnp.minimum(lam_re.astype(F32), -1e-4)
    li = lam_im.astype(F32)
    step = jnp.exp(log_step.astype(F32))[:, None]
    dr, di = lr * step, li * step
    tau = jnp.arange(L + 1, dtype=F32)
    mag = jnp.exp(dr[..., None] * tau)
    pw_re, pw_im = mag * jnp.cos(di[..., None] * tau), mag * jnp.sin(di[..., None] * tau)
    a_re, a_im = pw_re[..., 1], pw_im[..., 1]
    den = lr * lr + li * li
    f_re = ((a_re - 1.0) * lr + a_im * li) / den
    f_im = (a_im * lr - (a_re - 1.0) * li) / den
    br, bi = b_re.astype(F32), b_im.astype(F32)
    bb_re = f_re[..., None] * br - f_im[..., None] * bi
    bb_im = f_re[..., None] * bi + f_im[..., None] * br
    cr, ci = c_re.astype(F32), c_im.astype(F32)
    col = jnp.arange(P * L)
    pick_tau = (jnp.arange(L)[:, None] == L - 1 - col[None, :] % L).astype(F32)
    pick_p = (jnp.arange(P)[:, None] == col[None, :] // L).astype(F32)
    rv_re = jnp.einsum('gnt,tx->gnx', pw_re[..., :L], pick_tau, precision=hp)
    rv_im = jnp.einsum('gnt,tx->gnx', pw_im[..., :L], pick_tau, precision=hp)
    bx_re = jnp.einsum('gnp,px->gnx', bb_re, pick_p, precision=hp)
    bx_im = jnp.einsum('gnp,px->gnx', bb_im, pick_p, precision=hp)
    g_re, g_im = rv_re * bx_re - rv_im * bx_im, rv_re * bx_im + rv_im * bx_re
    g_t = jnp.concatenate([g_re, g_im], axis=1)
    taps = (jnp.einsum('gpn,gnx->gpx', cr, g_re, precision=hp)
            - jnp.einsum('gpn,gnx->gpx', ci, g_im, precision=hp))
    tap0 = (col[None, :] == jnp.arange(P)[:, None] * L + L - 1).astype(F32)
    taps = taps + d_skip.astype(F32).reshape(G, P, 1) * tap0
    cr, ci = cr[:, :, None, :], ci[:, :, None, :]
    nx_re = pw_re[..., 1:].transpose(0, 2, 1)[:, None]
    nx_im = pw_im[..., 1:].transpose(0, 2, 1)[:, None]
    c_t = jnp.concatenate([cr * nx_re - ci * nx_im, -(cr * nx_im + ci * nx_re)],
                          axis=-1).reshape(G, P * L, 2 * N)
    n_steps = max(1, (n_chunks - 1).bit_length())
    span = (L * 2.0 ** jnp.arange(n_steps, dtype=F32))[None, :, None]
    mag = jnp.exp(span * dr[:, None, :])
    a_pow = jnp.stack([mag * jnp.cos(span * di[:, None, :]), mag * jnp.sin(span * di[:, None, :])], axis=-1)
    return taps, g_t.astype(BF16), c_t.astype(BF16), a_pow


def _mix_out_kernel(x_ref, attn_ref, y_ref, gw_ref, gb_ref, wo_ref, g_ref, b_ref, o_ref):
    aw = attn_ref.shape[1]
    slab = x_ref.shape[0] // EPILOGUE_SLABS
    for r in range(EPILOGUE_SLABS):
        rows = slice(r * slab, (r + 1) * slab)
        yb = y_ref[rows, :]
        z = jnp.dot(yb, gw_ref[...], preferred_element_type=F32) + gb_ref[...]
        y = yb.astype(F32) * jax.nn.sigmoid(z)
        mixed = (jnp.dot(attn_ref[rows, :], wo_ref[:aw, :], preferred_element_type=F32)
                 + jnp.dot(y.astype(BF16), wo_ref[aw:, :], preferred_element_type=F32))
        o_ref[rows, :] = _layer_norm(ALPHA * x_ref[rows, :] + mixed, g_ref[...], b_ref[...])


def _mix_out(x, attn, y_ssm, glu_w, glu_b, w_out, g, b):
    t, d = x.shape
    tm = min(PROJ_TM, t)
    aw, sw = attn.shape[1], y_ssm.shape[1]
    row = lambda i: (i, 0)
    return pl.pallas_call(
        _mix_out_kernel,
        grid=(t // tm,),
        in_specs=[
            pl.BlockSpec((tm, d), row),
            pl.BlockSpec((tm, aw), row),
            pl.BlockSpec((tm, sw), row),
            _resident((sw, sw)),
            _resident((1, sw)),
            _resident((d, d)),
            _resident((1, d)),
            _resident((1, d)),
        ],
        out_specs=pl.BlockSpec((tm, d), row),
        out_shape=jax.ShapeDtypeStruct((t, d), F32),
        compiler_params=_params("parallel"),
        name="mix_out",
    )(x, attn, y_ssm, glu_w, glu_b, w_out, g, b)


def _mem_fold_kernel(m_ref, wq_ref, wk_ref, wv_ref, wo_ref, qk_ref, vo_ref):
    mb = m_ref[0].astype(BF16)
    k = jnp.dot(mb, wk_ref[...], preferred_element_type=F32).astype(BF16)
    v = jnp.dot(mb, wv_ref[...], preferred_element_type=F32).astype(BF16)
    qk = lax.dot_general(wq_ref[...], k, (((1,), (1,)), ((), ())),
                         preferred_element_type=F32)
    qk_ref[0] = (qk * (XATTN_HEAD_DIM ** -0.5)).astype(BF16)
    vo_ref[0] = jnp.dot(v, wo_ref[...], preferred_element_type=F32).astype(BF16)


def _mem_fold(mem, wq, wk, wv, wo):
    batch, n_mem, d = mem.shape
    hd = XATTN_HEAD_DIM
    col = lambda b, h: (0, h)
    return pl.pallas_call(
        _mem_fold_kernel,
        grid=(batch, XATTN_HEADS),
        in_specs=[pl.BlockSpec((1, n_mem, d), lambda b, h: (b, 0, 0)),
                  pl.BlockSpec((d, hd), col),
                  pl.BlockSpec((d, hd), col),
                  pl.BlockSpec((d, hd), col),
                  pl.BlockSpec((hd, d), lambda b, h: (h, 0))],
        out_specs=[pl.BlockSpec((1, d, n_mem), lambda b, h: (b, 0, h)),
                   pl.BlockSpec((1, n_mem, d), lambda b, h: (b, h, 0))],
        out_shape=[jax.ShapeDtypeStruct((batch, d, XATTN_HEADS * n_mem), BF16),
                   jax.ShapeDtypeStruct((batch, XATTN_HEADS * n_mem, d), BF16)],
        compiler_params=_params("parallel", "parallel"),
        name="mem_fold",
    )(mem, wq, wk, wv, wo)


def _xattn_ln_kernel(x_ref, qk_ref, vo_ref, g_ref, b_ref, o_ref, *, n_mem):
    x = x_ref[...]
    s = jnp.dot(x.astype(BF16), qk_ref[0], preferred_element_type=F32)
    probs = []
    for h in range(XATTN_HEADS):
        sh = s[:, h * n_mem:(h + 1) * n_mem]
        e = jnp.exp(sh - jnp.max(sh, axis=-1, keepdims=True))
        probs.append((e / jnp.sum(e, axis=-1, keepdims=True)).astype(BF16))
    out = jnp.dot(jnp.concatenate(probs, axis=1), vo_ref[0], preferred_element_type=F32)
    o_ref[...] = _layer_norm(ALPHA * x + out, g_ref[...], b_ref[...])


def _xattn_ln(x, qk, vo, g, b, seq):
    t, d = x.shape
    tm = min(PROJ_TM, seq)
    per_b = seq // tm
    hm = qk.shape[2]
    kern = functools.partial(_xattn_ln_kernel, n_mem=hm // XATTN_HEADS)
    return pl.pallas_call(
        kern,
        grid=(t // tm,),
        in_specs=[
            pl.BlockSpec((tm, d), lambda i: (i, 0)),
            pl.BlockSpec((1, d, hm), lambda i: (i // per_b, 0, 0)),
            pl.BlockSpec((1, hm, d), lambda i: (i // per_b, 0, 0)),
            _resident((1, d)),
            _resident((1, d)),
        ],
        out_specs=pl.BlockSpec((tm, d), lambda i: (i, 0)),
        out_shape=jax.ShapeDtypeStruct((t, d), F32),
        compiler_params=_params("parallel"),
        name="xattn_ln",
    )(x, qk, vo, g, b)


def kernel(x, mem, ffn1_w_gate, ffn1_w_up, ffn1_w_down, ln1_g, ln1_b, w_in, lambda_q1, lambda_k1, lambda_q2, lambda_k2, diff_norm_g, ssm_lambda_re, ssm_lambda_im, ssm_log_step, ssm_b_re, ssm_b_im, ssm_c_re, ssm_c_im, ssm_d, ssm_glu_w, ssm_glu_b, w_out, ln2_g, ln2_b, xattn_w_q, xattn_w_k, xattn_w_v, xattn_w_o, ln3_g, ln3_b, ffn2_w_gate, ffn2_w_up, ffn2_w_down, ln4_g, ln4_b):
    batch, seq, d = x.shape
    t = batch * seq
    aw = ATTN_WIDTH
    L, P, G = S5_CHUNK, SSM_GROUP, SSM_GROUPS
    n_chunks = seq // L
    xf = x.reshape(t, d)
    row = lambda a: a.reshape(1, -1).astype(F32)
    s5_ops = jax.vmap(functools.partial(_s5_operators, n_chunks=n_chunks))(
        ssm_lambda_re, ssm_lambda_im, ssm_log_step, ssm_b_re, ssm_b_im, ssm_c_re, ssm_c_im, ssm_d)

    for l in range(DEPTH):
        lam_init = 0.8 - 0.6 * math.exp(-0.3 * l)
        xf = _ffn_ln(xf, ffn1_w_gate[l].astype(BF16), ffn1_w_up[l].astype(BF16),
                     ffn1_w_down[l].astype(BF16), row(ln1_g[l]), row(ln1_b[l]))

        k_nat, qvut = _in_proj(xf, w_in[l].astype(BF16), w_in[l].T.astype(BF16))

        lam_p = jnp.stack([lambda_q1[l], lambda_k1[l], lambda_q2[l], lambda_k2[l]]).astype(F32)
        attn = _diff_attn(qvut, k_nat, lam_p, diff_norm_g[l].reshape(-1, 1).astype(F32), lam_init, batch, seq)

        ut_g = (qvut[2 * aw:].reshape(G * P, batch * n_chunks, L)
                .swapaxes(1, 2).reshape(G, P * L, batch * n_chunks))
        yt_g = _s5_conv(ut_g, *s5_ops, l, batch)
        y_ssm = yt_g.reshape(G * P, L, batch * n_chunks).transpose(2, 1, 0).reshape(t, G * P)

        xf = _mix_out(xf, attn, y_ssm, ssm_glu_w[l].astype(BF16),
                      row(ssm_glu_b[l]), w_out[l].astype(BF16), row(ln2_g[l]), row(ln2_b[l]))

        qk, vo = _mem_fold(mem, xattn_w_q[l].astype(BF16), xattn_w_k[l].astype(BF16),
                           xattn_w_v[l].astype(BF16), xattn_w_o[l].astype(BF16))
        xf = _xattn_ln(xf, qk, vo, row(ln3_g[l]), row(ln3_b[l]), seq)

        xf = _ffn_ln(xf, ffn2_w_gate[l].astype(BF16), ffn2_w_up[l].astype(BF16),
                     ffn2_w_down[l].astype(BF16), row(ln4_g[l]), row(ln4_b[l]))
    return xf.reshape(batch, seq, d)
```

```python
import functools
import math

import jax
import jax.numpy as jnp
from jax import lax
from jax.experimental import pallas as pl
from jax.experimental.pallas import tpu as pltpu

F32 = jnp.float32
BF16 = jnp.bfloat16

D_MODEL = 2048
DEPTH = 4
CHUNK = 64
ATTN_WIDTH = D_MODEL // 2
SSM_WIDTH = D_MODEL - ATTN_WIDTH
DIFF_HEAD_DIM = 64
DIFF_V_DIM = 2 * DIFF_HEAD_DIM
DIFF_HEADS = ATTN_WIDTH // DIFF_V_DIM
SSM_GROUP = 16
SSM_GROUPS = SSM_WIDTH // SSM_GROUP
SSM_STATE = 64
XATTN_HEADS = 4
XATTN_HEAD_DIM = D_MODEL // XATTN_HEADS
ALPHA = (2 * DEPTH) ** 0.25
LN_EPS = 1e-5
RMS_EPS = 1e-5
NEG_BIG = -1e30
LOG2_E = 1.4426950408889634

V7X_VMEM_LIMIT_BYTES = 56 * 1024 * 1024
F32_SUBLANES = 8

FFN_TM = 512
FFN_TF = 512
FFN_STEP_CHUNKS = 2
PROJ_TM = 512
EPILOGUE_SLABS = 2
ATTN_TQ = 1024
ATTN_TK = 512
ATTN_KS = 256
ATTN_QS = 512
ATTN_LOOKAHEAD = 2
ATTN_SHIFT_SLACK = 64.0
S5_CHUNK = 64


def _params(*sem):
    return pltpu.CompilerParams(dimension_semantics=sem,
                                vmem_limit_bytes=V7X_VMEM_LIMIT_BYTES)


def _resident(shape):
    return pl.BlockSpec(shape, lambda *_: (0,) * len(shape), pipeline_mode=pl.Buffered(1))


def _layer_norm(y, g, b):
    mu = jnp.mean(y, axis=-1, keepdims=True)
    yc = y - mu
    var = jnp.mean(yc * yc, axis=-1, keepdims=True)
    return yc * lax.rsqrt(var + LN_EPS) * g + b


def _ffn_ln_kernel(x_ref, wg_ref, wu_ref, wd_ref, g_ref, b_ref, o_ref, xb_ref, acc_ref, *, last_chunks):
    f = pl.program_id(1)
    last = pl.num_programs(1) - 1
    tf = FFN_TF

    @pl.when(f == 0)
    def _():
        xb_ref[...] = x_ref[...].astype(BF16)
        acc_ref[...] = jnp.zeros_like(acc_ref)

    def chunks(n):
        xb = xb_ref[...]
        total = None
        for c in range(n):
            cols = slice(c * tf, (c + 1) * tf)
            gate = jnp.dot(xb, wg_ref[:, cols], preferred_element_type=F32)
            up = jnp.dot(xb, wu_ref[:, cols], preferred_element_type=F32)
            h = (gate * jax.nn.sigmoid(gate) * up).astype(BF16)
            part = jnp.dot(h, wd_ref[cols, :], preferred_element_type=F32)
            total = part if total is None else total + part
        return total

    @pl.when(f < last)
    def _():
        acc_ref[...] += chunks(FFN_STEP_CHUNKS)

    @pl.when(f == last)
    def _():
        y = ALPHA * x_ref[...] + 0.5 * (acc_ref[...] + chunks(last_chunks))
        o_ref[...] = _layer_norm(y, g_ref[...], b_ref[...])


def _ffn_weights(wg, wu, wd):
    pad = -wg.shape[1] % (FFN_TF * FFN_STEP_CHUNKS)
    cols = lambda w: jnp.pad(w.astype(BF16), ((0, 0), (0, pad)))
    return cols(wg), cols(wu), jnp.pad(wd.astype(BF16), ((0, pad), (0, 0))), wg.shape[1]


def _ffn_ln(x, wg, wu, wd, d_ff, g, b):
    t, d = x.shape
    tm, ts = min(FFN_TM, t), FFN_TF * FFN_STEP_CHUNKS
    steps = wg.shape[1] // ts
    last_chunks = d_ff // FFN_TF - (steps - 1) * FFN_STEP_CHUNKS
    kern = functools.partial(_ffn_ln_kernel, last_chunks=last_chunks)
    return pl.pallas_call(
        kern,
        grid=(t // tm, steps),
        in_specs=[
            pl.BlockSpec((tm, d), lambda i, j: (i, 0)),
            pl.BlockSpec((d, ts), lambda i, j: (0, j)),
            pl.BlockSpec((d, ts), lambda i, j: (0, j)),
            pl.BlockSpec((ts, d), lambda i, j: (j, 0)),
            pl.BlockSpec((1, d), lambda i, j: (0, 0)),
            pl.BlockSpec((1, d), lambda i, j: (0, 0)),
        ],
        out_specs=pl.BlockSpec((tm, d), lambda i, j: (i, 0)),
        out_shape=jax.ShapeDtypeStruct((t, d), F32),
        scratch_shapes=[pltpu.VMEM((tm, d), BF16), pltpu.VMEM((tm, d), F32)],
        compiler_params=_params("parallel", "arbitrary"),
        name="ffn_ln",
    )(x, wg, wu, wd, g, b)


def _in_proj_kernel(x_ref, wk_ref, wqt_ref, wvut_ref, k_ref, qvut_ref):
    xb = x_ref[...].astype(BF16)
    k_ref[...] = jnp.dot(xb, wk_ref[...], preferred_element_type=F32).astype(BF16)
    nt = (((1,), (1,)), ((), ()))
    nq = wqt_ref.shape[0]
    qvut_ref[:nq, :] = lax.dot_general(wqt_ref[...], xb, nt, preferred_element_type=F32).astype(BF16)
    qvut_ref[nq:, :] = lax.dot_general(wvut_ref[...], xb, nt, preferred_element_type=F32).astype(BF16)


def _in_proj(x, w, w_t):
    t, d = x.shape
    tm = min(PROJ_TM, t)
    aw = ATTN_WIDTH
    n_t = w.shape[1] - aw
    once = pl.Buffered(1)
    return pl.pallas_call(
        _in_proj_kernel,
        grid=(t // tm,),
        in_specs=[
            pl.BlockSpec((tm, d), lambda i: (i, 0)),
            pl.BlockSpec((d, aw), lambda i: (0, 1), pipeline_mode=once),
            pl.BlockSpec((aw, d), lambda i: (0, 0), pipeline_mode=once),
            pl.BlockSpec((2 * aw, d), lambda i: (1, 0), pipeline_mode=once),
        ],
        out_specs=[
            pl.BlockSpec((tm, aw), lambda i: (i, 0)),
            pl.BlockSpec((n_t, tm), lambda i: (0, i)),
        ],
        out_shape=[jax.ShapeDtypeStruct((t, aw), BF16),
                   jax.ShapeDtypeStruct((n_t, t), BF16)],
        compiler_params=_params("parallel"),
        name="in_proj",
    )(x, w, w_t, w_t)


def _diff_attn_kernel(qt_ref, k_ref, vt_ref, lam_ref, g_ref, o_ref,
                      acc_ref, m_ref, l_ref, ls_ref, *, tq, tk, lam_init):
    i = pl.program_id(2)
    dh = DIFF_HEAD_DIM
    ks, qs = ATTN_KS, ATTN_QS
    nkb = tq // tk
    n_strips, n_sub = 2 * tq // qs, tk // ks

    qt = qt_ref[...].astype(F32) * (dh ** -0.5 * LOG2_E)
    row = lax.broadcasted_iota(jnp.int32, qt.shape, 0)
    qz = jnp.concatenate([jnp.where(row < dh, qt, 0.0),
                          jnp.where(row >= dh, qt, 0.0)], axis=1).astype(BF16)

    def init():
        m_ref[...] = jnp.full(m_ref.shape, NEG_BIG, F32)
        l_ref[...] = jnp.zeros_like(l_ref)
        acc_ref[...] = jnp.zeros_like(acc_ref)
        ls_ref[...] = jnp.zeros_like(ls_ref)

    def chunk_mask(shape, k_lo, q_lo):
        kc = (k_lo + lax.broadcasted_iota(jnp.int32, shape, 0)) // CHUNK
        qc = (q_lo + lax.broadcasted_iota(jnp.int32, shape, 1)) // CHUNK
        return kc <= qc

    def exact_step(j, koff):
        k0 = pl.multiple_of(j * tk, tk)
        kb = k_ref[pl.ds(k0, tk), :]
        s = jnp.dot(kb, qz, preferred_element_type=F32)
        if koff is not None:
            keep = chunk_mask((tk, tq), koff, 0)
            s = jnp.where(jnp.concatenate([keep, keep], axis=1), s, NEG_BIG)
        m_old = m_ref[...]
        m_new = jnp.maximum(m_old, jnp.max(s, axis=0, keepdims=True))
        corr = jnp.exp2(m_old - m_new)
        p = jnp.exp2(s - m_new)
        l_ref[...] = corr * l_ref[...] + jnp.sum(p, axis=0, keepdims=True)
        vtb = vt_ref[:, pl.ds(k0, tk)]
        acc_ref[...] = corr * acc_ref[...] + jnp.dot(vtb, p.astype(BF16),
                                                     preferred_element_type=F32)
        m_ref[...] = m_new

    def scores(j, koff, c, r):
        q_lo = (c * qs) % tq
        kb = k_ref[pl.ds(pl.multiple_of(j * tk, tk) + r * ks, ks), :]
        s = jnp.dot(kb, qz[:, c * qs:(c + 1) * qs], preferred_element_type=F32)
        if koff is not None and (koff + r * ks + ks - 1) // CHUNK > q_lo // CHUNK:
            s = jnp.where(chunk_mask(s.shape, koff + r * ks, q_lo), s, NEG_BIG)
        return s

    def prologue():
        init()
        s = jnp.dot(k_ref[0:CHUNK, :], qz, preferred_element_type=F32)
        m_ref[...] = jnp.max(s, axis=0, keepdims=True)

    def fast_steps(blocks):
        def above_diagonal(c, b, r):
            koff = blocks[b][1]
            return koff is not None and (koff + r * ks) // CHUNK > ((c * qs) % tq + qs - 1) // CHUNK

        tiles = [(c, b, r) for c in range(n_strips) for b in range(len(blocks)) for r in range(n_sub)
                 if not above_diagonal(c, b, r)]
        score = lambda c, b, r: scores(blocks[b][0], blocks[b][1], c, r)
        in_flight = [score(*t) for t in tiles[:ATTN_LOOKAHEAD]]
        for n, (c, b, r) in enumerate(tiles):
            lanes = slice(c * qs, (c + 1) * qs)
            s = in_flight.pop(0)
            if n + ATTN_LOOKAHEAD < len(tiles):
                in_flight.append(score(*tiles[n + ATTN_LOOKAHEAD]))
            if n == 0 or tiles[n - 1][0] != c:
                shift = m_ref[:, lanes]
                lsum = ls_ref[:, lanes]
                pv = None
            e = jnp.exp2(s - shift)
            lsum = lsum + jnp.sum(e.reshape(ks // F32_SUBLANES, F32_SUBLANES, qs), axis=0)
            vtb = vt_ref[:, pl.ds(pl.multiple_of(blocks[b][0] * tk, tk) + r * ks, ks)]
            d = jnp.dot(vtb, e.astype(BF16), preferred_element_type=F32)
            pv = d if pv is None else pv + d
            if n + 1 == len(tiles) or tiles[n + 1][0] != c:
                acc_ref[:, lanes] += pv
                ls_ref[:, lanes] = lsum

    diagonal = [(nkb * i + d, d * tk) for d in range(nkb)]

    def exact_body(j, carry):
        exact_step(j, None)
        return carry

    def fast_body(p, carry):
        fast_steps([(2 * nkb * p + d, None) for d in range(2 * nkb)])
        return carry

    prologue()

    @pl.when(i > 0)
    def _():
        lax.fori_loop(0, i // 2, fast_body, 0)

        @pl.when(i % 2 == 1)
        def _():
            fast_steps([(nkb * (i - 1) + d, None) for d in range(nkb)])

    fast_steps(diagonal)
    in_range = jnp.max(ls_ref[...]) <= 2.0 ** ATTN_SHIFT_SLACK

    @pl.when(jnp.logical_not(in_range))
    def _():
        init()
        lax.fori_loop(0, nkb * i, exact_body, 0)
        for j, koff in diagonal:
            exact_step(j, koff)

    lam_p = lam_ref[...]
    lam = (jnp.exp(jnp.sum(lam_p[0:1] * lam_p[1:2], axis=-1, keepdims=True))
           - jnp.exp(jnp.sum(lam_p[2:3] * lam_p[3:4], axis=-1, keepdims=True))
           + lam_init)
    acc = acc_ref[...]
    l = l_ref[...] + jnp.sum(ls_ref[...], axis=0, keepdims=True)
    ot = acc[:, :tq] / l[:, :tq] - lam * (acc[:, tq:] / l[:, tq:])
    ot = ot * lax.rsqrt(jnp.mean(ot * ot, axis=0, keepdims=True) + RMS_EPS)
    o_ref[...] = (ot * g_ref[...] * (1.0 - lam_init)).T.astype(o_ref.dtype)


def _diff_attn(qvt, ku, lam_p, g, lam_init, batch, seq):
    tk = min(ATTN_TK, seq)
    tq = min(ATTN_TQ, seq)
    assert tq % tk == 0 and tk % CHUNK == 0 and seq % tq == 0
    nq = seq // tq
    hh = DIFF_HEADS
    dv = DIFF_V_DIM
    kern = functools.partial(_diff_attn_kernel, tq=tq, tk=tk, lam_init=lam_init)
    return pl.pallas_call(
        kern,
        grid=(batch, hh, nq),
        in_specs=[
            pl.BlockSpec((dv, tq), lambda b, h, i: (h, b * nq + i)),
            pl.BlockSpec((seq, dv), lambda b, h, i: (b, h)),
            pl.BlockSpec((dv, seq), lambda b, h, i: (hh + h, b)),
            pl.BlockSpec((4, DIFF_HEAD_DIM), lambda b, h, i: (0, 0)),
            pl.BlockSpec((dv, 1), lambda b, h, i: (0, 0)),
        ],
        out_specs=pl.BlockSpec((tq, dv), lambda b, h, i: (b * nq + i, h)),
        out_shape=jax.ShapeDtypeStruct((batch * seq, hh * dv), BF16),
        scratch_shapes=[pltpu.VMEM((dv, 2 * tq), F32),
                        pltpu.VMEM((1, 2 * tq), F32),
                        pltpu.VMEM((1, 2 * tq), F32),
                        pltpu.VMEM((F32_SUBLANES, 2 * tq), F32)],
        compiler_params=_params("parallel", "parallel", "arbitrary"),
        name="diff_attn",
    )(qvt, ku, qvt, lam_p, g)


def _s5_kernel(ut_ref, taps_ref, gt_ref, ct_ref, apow_ref, y_ref, mt_ref, *, batch):
    L, P, N = S5_CHUNK, SSM_GROUP, SSM_STATE
    ut = ut_ref[0]
    cols = ut.shape[1]
    per_b = cols // batch

    x = jnp.dot(gt_ref[0], ut, preferred_element_type=F32)
    x_re, x_im = x[:N], x[N:]
    chunk = lax.broadcasted_iota(jnp.int32, (N, cols), 1) % per_b
    for k in range(apow_ref.shape[1]):
        sh = 1 << k
        w_re, w_im = apow_ref[0, k, :, 0:1], apow_ref[0, k, :, 1:2]
        r_re = jnp.where(chunk >= sh, pltpu.roll(x_re, sh, 1), 0.0)
        r_im = jnp.where(chunk >= sh, pltpu.roll(x_im, sh, 1), 0.0)
        x_re, x_im = x_re + (w_re * r_re - w_im * r_im), x_im + (w_re * r_im + w_im * r_re)
    st = jnp.concatenate([jnp.where(chunk >= 1, pltpu.roll(x_re, 1, 1), 0.0),
                          jnp.where(chunk >= 1, pltpu.roll(x_im, 1, 1), 0.0)], axis=0)
    hi = st.astype(BF16)
    lo = (st - hi.astype(F32)).astype(BF16)

    keep = (lax.broadcasted_iota(jnp.int32, (L, P * L), 1) % L
            <= lax.broadcasted_iota(jnp.int32, (L, P * L), 0))
    for p in range(P):
        base = jnp.broadcast_to(taps_ref[0, p:p + 1, :], (L, P * L))
        rows = pltpu.roll(base, P * L - (L - 1), 1, stride=1, stride_axis=0)
        mt_ref[p * L:(p + 1) * L, :] = jnp.where(keep, rows, 0.0).astype(BF16)

    ct = ct_ref[0]
    y = (jnp.dot(mt_ref[...], ut, preferred_element_type=F32)
         + jnp.dot(ct, hi, preferred_element_type=F32)
         + jnp.dot(ct, lo, preferred_element_type=F32))
    y_ref[0] = jax.nn.gelu(y).astype(y_ref.dtype)


def _s5_conv(ut_g, taps, g_t, c_t, a_pow, layer, batch):
    groups, lp, cols = ut_g.shape
    n = SSM_STATE
    kern = functools.partial(_s5_kernel, batch=batch)
    per_group = lambda *shape: pl.BlockSpec((1,) + shape, lambda g: (g,) + (0,) * len(shape))
    of_layer = lambda *shape: pl.BlockSpec((None, 1) + shape, lambda g: (layer, g) + (0,) * len(shape))
    return pl.pallas_call(
        kern,
        grid=(groups,),
        in_specs=[per_group(lp, cols), of_layer(SSM_GROUP, lp), of_layer(2 * n, lp),
                  of_layer(lp, 2 * n), of_layer(*a_pow.shape[2:])],
        out_specs=per_group(lp, cols),
        out_shape=jax.ShapeDtypeStruct((groups, lp, cols), BF16),
        scratch_shapes=[pltpu.VMEM((lp, lp), BF16)],
        compiler_params=_params("parallel"),
        name="s5_conv",
    )(ut_g, taps, g_t, c_t, a_pow)


def _s5_operators(lam_re, lam_im, log_step, b_re, b_im, c_re, c_im, d_skip, n_chunks):
    hp = lax.Precision.HIGHEST
    L, P, G, N = S5_CHUNK, SSM_GROUP, SSM_GROUPS, SSM_STATE
    lr = jnp.minimum(lam_re.astype(F32), -1e-4)
    li = lam_im.astype(F32)
    step = jnp.exp(log_step.astype(F32))[:, None]
    dr, di = lr * step, li * step
    tau = jnp.arange(L + 1, dtype=F32)
    mag = jnp.exp(dr[..., None] * tau)
    pw_re, pw_im = mag * jnp.cos(di[..., None] * tau), mag * jnp.sin(di[..., None] * tau)
    a_re, a_im = pw_re[..., 1], pw_im[..., 1]
    den = lr * lr + li * li
    f_re = ((a_re - 1.0) * lr + a_im * li) / den
    f_im = (a_im * lr - (a_re - 1.0) * li) / den
    br, bi = b_re.astype(F32), b_im.astype(F32)
    bb_re = f_re[..., None] * br - f_im[..., None] * bi
    bb_im = f_re[..., None] * bi + f_im[..., None] * br
    cr, ci = c_re.astype(F32), c_im.astype(F32)
    col = jnp.arange(P * L)
    pick_tau = (jnp.arange(L)[:, None] == L - 1 - col[None, :] % L).astype(F32)
    pick_p = (jnp.arange(P)[:, None] == col[None, :] // L).astype(F32)
    rv_re = jnp.einsum('gnt,tx->gnx', pw_re[..., :L], pick_tau, precision=hp)
    rv_im = jnp.einsum('gnt,tx->gnx', pw_im[..., :L], pick_tau, precision=hp)
    bx_re = jnp.einsum('gnp,px->gnx', bb_re, pick_p, precision=hp)
    bx_im = jnp.einsum('gnp,px->gnx', bb_im, pick_p, precision=hp)
    g_re, g_im = rv_re * bx_re - rv_im * bx_im, rv_re * bx_im + rv_im * bx_re
    g_t = jnp.concatenate([g_re, g_im], axis=1)
    taps = (jnp.einsum('gpn,gnx->gpx', cr, g_re, precision=hp)
            - jnp.einsum('gpn,gnx->gpx', ci, g_im, precision=hp))
    tap0 = (col[None, :] == jnp.arange(P)[:, None] * L + L - 1).astype(F32)
    taps = taps + d_skip.astype(F32).reshape(G, P, 1) * tap0
    cr, ci = cr[:, :, None, :], ci[:, :, None, :]
    nx_re = pw_re[..., 1:].transpose(0, 2, 1)[:, None]
    nx_im = pw_im[..., 1:].transpose(0, 2, 1)[:, None]
    c_t = jnp.concatenate([cr * nx_re - ci * nx_im, -(cr * nx_im + ci * nx_re)],
                          axis=-1).reshape(G, P * L, 2 * N)
    n_steps = max(1, (n_chunks - 1).bit_length())
    span = (L * 2.0 ** jnp.arange(n_steps, dtype=F32))[None, :, None]
    mag = jnp.exp(span * dr[:, None, :])
    a_pow = jnp.stack([mag * jnp.cos(span * di[:, None, :]), mag * jnp.sin(span * di[:, None, :])], axis=-1)
    return taps, g_t.astype(BF16), c_t.astype(BF16), a_pow


def _mix_out_kernel(x_ref, attn_ref, y_ref, gw_ref, gb_ref, wo_ref, g_ref, b_ref, o_ref):
    aw = attn_ref.shape[1]
    slab = x_ref.shape[0] // EPILOGUE_SLABS
    for r in range(EPILOGUE_SLABS):
        rows = slice(r * slab, (r + 1) * slab)
        yb = y_ref[rows, :]
        z = jnp.dot(yb, gw_ref[...], preferred_element_type=F32) + gb_ref[...]
        y = yb.astype(F32) * jax.nn.sigmoid(z)
        mixed = (jnp.dot(attn_ref[rows, :], wo_ref[:aw, :], preferred_element_type=F32)
                 + jnp.dot(y.astype(BF16), wo_ref[aw:, :], preferred_element_type=F32))
        o_ref[rows, :] = _layer_norm(ALPHA * x_ref[rows, :] + mixed, g_ref[...], b_ref[...])


def _mix_out(x, attn, y_ssm, glu_w, glu_b, w_out, g, b):
    t, d = x.shape
    tm = min(PROJ_TM, t)
    aw, sw = attn.shape[1], y_ssm.shape[1]
    row = lambda i: (i, 0)
    return pl.pallas_call(
        _mix_out_kernel,
        grid=(t // tm,),
        in_specs=[
            pl.BlockSpec((tm, d), row),
            pl.BlockSpec((tm, aw), row),
            pl.BlockSpec((tm, sw), row),
            _resident((sw, sw)),
            _resident((1, sw)),
            _resident((d, d)),
            _resident((1, d)),
            _resident((1, d)),
        ],
        out_specs=pl.BlockSpec((tm, d), row),
        out_shape=jax.ShapeDtypeStruct((t, d), F32),
        compiler_params=_params("parallel"),
        name="mix_out",
    )(x, attn, y_ssm, glu_w, glu_b, w_out, g, b)


def _mem_fold_kernel(m_ref, wq_ref, wk_ref, wv_ref, wo_ref, qk_ref, vo_ref):
    mb = m_ref[0].astype(BF16)
    k = jnp.dot(mb, wk_ref[...], preferred_element_type=F32).astype(BF16)
    v = jnp.dot(mb, wv_ref[...], preferred_element_type=F32).astype(BF16)
    qk = lax.dot_general(wq_ref[...], k, (((1,), (1,)), ((), ())),
                         preferred_element_type=F32)
    qk_ref[0] = (qk * (XATTN_HEAD_DIM ** -0.5)).astype(BF16)
    vo_ref[0] = jnp.dot(v, wo_ref[...], preferred_element_type=F32).astype(BF16)


def _mem_fold(mem, wq, wk, wv, wo):
    batch, n_mem, d = mem.shape
    hd = XATTN_HEAD_DIM
    col = lambda b, h: (0, h)
    return pl.pallas_call(
        _mem_fold_kernel,
        grid=(batch, XATTN_HEADS),
        in_specs=[pl.BlockSpec((1, n_mem, d), lambda b, h: (b, 0, 0)),
                  pl.BlockSpec((d, hd), col),
                  pl.BlockSpec((d, hd), col),
                  pl.BlockSpec((d, hd), col),
                  pl.BlockSpec((hd, d), lambda b, h: (h, 0))],
        out_specs=[pl.BlockSpec((1, d, n_mem), lambda b, h: (b, 0, h)),
                   pl.BlockSpec((1, n_mem, d), lambda b, h: (b, h, 0))],
        out_shape=[jax.ShapeDtypeStruct((batch, d, XATTN_HEADS * n_mem), BF16),
                   jax.ShapeDtypeStruct((batch, XATTN_HEADS * n_mem, d), BF16)],
        compiler_params=_params("parallel", "parallel"),
        name="mem_fold",
    )(mem, wq, wk, wv, wo)


def _xattn_ln_kernel(x_ref, qk_ref, vo_ref, g_ref, b_ref, o_ref, *, n_mem):
    x = x_ref[...]
    s = jnp.dot(x.astype(BF16), qk_ref[0], preferred_element_type=F32)
    probs = []
    for h in range(XATTN_HEADS):
        sh = s[:, h * n_mem:(h + 1) * n_mem]
        e = jnp.exp(sh - jnp.max(sh, axis=-1, keepdims=True))
        probs.append((e / jnp.sum(e, axis=-1, keepdims=True)).astype(BF16))
    out = jnp.dot(jnp.concatenate(probs, axis=1), vo_ref[0], preferred_element_type=F32)
    o_ref[...] = _layer_norm(ALPHA * x + out, g_ref[...], b_ref[...])


def _xattn_ln(x, qk, vo, g, b, seq):
    t, d = x.shape
    tm = min(PROJ_TM, seq)
    per_b = seq // tm
    hm = qk.shape[2]
    kern = functools.partial(_xattn_ln_kernel, n_mem=hm // XATTN_HEADS)
    return pl.pallas_call(
        kern,
        grid=(t // tm,),
        in_specs=[
            pl.BlockSpec((tm, d), lambda i: (i, 0)),
            pl.BlockSpec((1, d, hm), lambda i: (i // per_b, 0, 0)),
            pl.BlockSpec((1, hm, d), lambda i: (i // per_b, 0, 0)),
            _resident((1, d)),
            _resident((1, d)),
        ],
        out_specs=pl.BlockSpec((tm, d), lambda i: (i, 0)),
        out_shape=jax.ShapeDtypeStruct((t, d), F32),
        compiler_params=_params("parallel"),
        name="xattn_ln",
    )(x, qk, vo, g, b)


def kernel(x, mem, ffn1_w_gate, ffn1_w_up, ffn1_w_down, ln1_g, ln1_b, w_in, lambda_q1, lambda_k1, lambda_q2, lambda_k2, diff_norm_g, ssm_lambda_re, ssm_lambda_im, ssm_log_step, ssm_b_re, ssm_b_im, ssm_c_re, ssm_c_im, ssm_d, ssm_glu_w, ssm_glu_b, w_out, ln2_g, ln2_b, xattn_w_q, xattn_w_k, xattn_w_v, xattn_w_o, ln3_g, ln3_b, ffn2_w_gate, ffn2_w_up, ffn2_w_down, ln4_g, ln4_b):
    batch, seq, d = x.shape
    t = batch * seq
    aw = ATTN_WIDTH
    L, P, G = S5_CHUNK, SSM_GROUP, SSM_GROUPS
    n_chunks = seq // L
    xf = x.reshape(t, d)
    row = lambda a: a.reshape(1, -1).astype(F32)
    s5_ops = jax.vmap(functools.partial(_s5_operators, n_chunks=n_chunks))(
        ssm_lambda_re, ssm_lambda_im, ssm_log_step, ssm_b_re, ssm_b_im, ssm_c_re, ssm_c_im, ssm_d)

    for l in range(DEPTH):
        lam_init = 0.8 - 0.6 * math.exp(-0.3 * l)
        xf = _ffn_ln(xf, *_ffn_weights(ffn1_w_gate[l], ffn1_w_up[l], ffn1_w_down[l]),
                     row(ln1_g[l]), row(ln1_b[l]))

        k_nat, qvut = _in_proj(xf, w_in[l].astype(BF16), w_in[l].T.astype(BF16))

        lam_p = jnp.stack([lambda_q1[l], lambda_k1[l], lambda_q2[l], lambda_k2[l]]).astype(F32)
        attn = _diff_attn(qvut, k_nat, lam_p, diff_norm_g[l].reshape(-1, 1).astype(F32), lam_init, batch, seq)

        ut_g = (qvut[2 * aw:].reshape(G * P, batch * n_chunks, L)
                .swapaxes(1, 2).reshape(G, P * L, batch * n_chunks))
        yt_g = _s5_conv(ut_g, *s5_ops, l, batch)
        y_ssm = yt_g.reshape(G * P, L, batch * n_chunks).transpose(2, 1, 0).reshape(t, G * P)

        xf = _mix_out(xf, attn, y_ssm, ssm_glu_w[l].astype(BF16),
                      row(ssm_glu_b[l]), w_out[l].astype(BF16), row(ln2_g[l]), row(ln2_b[l]))

        qk, vo = _mem_fold(mem, xattn_w_q[l].astype(BF16), xattn_w_k[l].astype(BF16),
                           xattn_w_v[l].astype(BF16), xattn_w_o[l].astype(BF16))
        xf = _xattn_ln(xf, qk, vo, row(ln3_g[l]), row(ln3_b[l]), seq)

        xf = _ffn_ln(xf, *_ffn_weights(ffn2_w_gate[l], ffn2_w_up[l], ffn2_w_down[l]),
                     row(ln4_g[l]), row(ln4_b[l]))
    return xf.reshape(batch, seq, d)
```

```python
import functools
import math

import jax
import jax.numpy as jnp
from jax import lax
from jax.experimental import pallas as pl
from jax.experimental.pallas import tpu as pltpu

F32 = jnp.float32
BF16 = jnp.bfloat16

D_MODEL = 2048
DEPTH = 4
CHUNK = 64
ATTN_WIDTH = D_MODEL // 2
SSM_WIDTH = D_MODEL - ATTN_WIDTH
DIFF_HEAD_DIM = 64
DIFF_V_DIM = 2 * DIFF_HEAD_DIM
DIFF_HEADS = ATTN_WIDTH // DIFF_V_DIM
SSM_GROUP = 16
SSM_GROUPS = SSM_WIDTH // SSM_GROUP
SSM_STATE = 64
XATTN_HEADS = 4
XATTN_HEAD_DIM = D_MODEL // XATTN_HEADS
ALPHA = (2 * DEPTH) ** 0.25
LN_EPS = 1e-5
RMS_EPS = 1e-5
NEG_BIG = -1e30
LOG2_E = 1.4426950408889634

V7X_VMEM_LIMIT_BYTES = 56 * 1024 * 1024
F32_SUBLANES = 8

FFN_TM = 512
FFN_TF = 512
FFN_STEP_CHUNKS = 2
PROJ_TM = 512
EPILOGUE_SLABS = 2
ATTN_TQ = 1024
ATTN_TK = 512
ATTN_KS = 256
ATTN_QS = 512
ATTN_LOOKAHEAD = 2
ATTN_SHIFT_SLACK = 64.0
S5_CHUNK = 64


def _params(*sem):
    return pltpu.CompilerParams(dimension_semantics=sem,
                                vmem_limit_bytes=V7X_VMEM_LIMIT_BYTES)


def _resident(shape):
    return pl.BlockSpec(shape, lambda *_: (0,) * len(shape), pipeline_mode=pl.Buffered(1))


def _layer_norm(y, g, b):
    mu = jnp.mean(y, axis=-1, keepdims=True)
    yc = y - mu
    var = jnp.mean(yc * yc, axis=-1, keepdims=True)
    return yc * lax.rsqrt(var + LN_EPS) * g + b


def _ffn_ln_kernel(x_ref, wg_ref, wu_ref, wd_ref, g_ref, b_ref, o_ref, xb_ref, acc_ref, *, last_chunks):
    f = pl.program_id(1)
    last = pl.num_programs(1) - 1
    tf = FFN_TF

    @pl.when(f == 0)
    def _():
        xb_ref[...] = x_ref[...].astype(BF16)
        acc_ref[...] = jnp.zeros_like(acc_ref)

    def chunks(n):
        xb = xb_ref[...]
        total = None
        for c in range(n):
            cols = slice(c * tf, (c + 1) * tf)
            gate = jnp.dot(xb, wg_ref[:, cols], preferred_element_type=F32)
            up = jnp.dot(xb, wu_ref[:, cols], preferred_element_type=F32)
            h = (gate * jax.nn.sigmoid(gate) * up).astype(BF16)
            part = jnp.dot(h, wd_ref[cols, :], preferred_element_type=F32)
            total = part if total is None else total + part
        return total

    @pl.when(f < last)
    def _():
        acc_ref[...] += chunks(FFN_STEP_CHUNKS)

    @pl.when(f == last)
    def _():
        y = ALPHA * x_ref[...] + 0.5 * (acc_ref[...] + chunks(last_chunks))
        o_ref[...] = _layer_norm(y, g_ref[...], b_ref[...])


def _ffn_ln(x, wg, wu, wd, g, b):
    t, d = x.shape
    d_ff = wg.shape[1]
    tm, ts = min(FFN_TM, t), FFN_TF * FFN_STEP_CHUNKS
    steps = pl.cdiv(d_ff, ts)
    last_chunks = d_ff // FFN_TF - (steps - 1) * FFN_STEP_CHUNKS
    kern = functools.partial(_ffn_ln_kernel, last_chunks=last_chunks)
    return pl.pallas_call(
        kern,
        grid=(t // tm, steps),
        in_specs=[
            pl.BlockSpec((tm, d), lambda i, j: (i, 0)),
            pl.BlockSpec((d, ts), lambda i, j: (0, j)),
            pl.BlockSpec((d, ts), lambda i, j: (0, j)),
            pl.BlockSpec((ts, d), lambda i, j: (j, 0)),
            pl.BlockSpec((1, d), lambda i, j: (0, 0)),
            pl.BlockSpec((1, d), lambda i, j: (0, 0)),
        ],
        out_specs=pl.BlockSpec((tm, d), lambda i, j: (i, 0)),
        out_shape=jax.ShapeDtypeStruct((t, d), F32),
        scratch_shapes=[pltpu.VMEM((tm, d), BF16), pltpu.VMEM((tm, d), F32)],
        compiler_params=_params("parallel", "arbitrary"),
        name="ffn_ln",
    )(x, wg, wu, wd, g, b)


def _in_proj_kernel(x_ref, wk_ref, wqt_ref, wvut_ref, k_ref, qvut_ref):
    xb = x_ref[...].astype(BF16)
    k_ref[...] = jnp.dot(xb, wk_ref[...], preferred_element_type=F32).astype(BF16)
    nt = (((1,), (1,)), ((), ()))
    nq = wqt_ref.shape[0]
    qvut_ref[:nq, :] = lax.dot_general(wqt_ref[...], xb, nt, preferred_element_type=F32).astype(BF16)
    qvut_ref[nq:, :] = lax.dot_general(wvut_ref[...], xb, nt, preferred_element_type=F32).astype(BF16)


def _in_proj(x, w, w_t):
    t, d = x.shape
    tm = min(PROJ_TM, t)
    aw = ATTN_WIDTH
    n_t = w.shape[1] - aw
    once = pl.Buffered(1)
    return pl.pallas_call(
        _in_proj_kernel,
        grid=(t // tm,),
        in_specs=[
            pl.BlockSpec((tm, d), lambda i: (i, 0)),
            pl.BlockSpec((d, aw), lambda i: (0, 1), pipeline_mode=once),
            pl.BlockSpec((aw, d), lambda i: (0, 0), pipeline_mode=once),
            pl.BlockSpec((2 * aw, d), lambda i: (1, 0), pipeline_mode=once),
        ],
        out_specs=[
            pl.BlockSpec((tm, aw), lambda i: (i, 0)),
            pl.BlockSpec((n_t, tm), lambda i: (0, i)),
        ],
        out_shape=[jax.ShapeDtypeStruct((t, aw), BF16),
                   jax.ShapeDtypeStruct((n_t, t), BF16)],
        compiler_params=_params("parallel"),
        name="in_proj",
    )(x, w, w_t, w_t)


def _diff_attn_kernel(qt_ref, k_ref, vt_ref, lam_ref, g_ref, o_ref,
                      acc_ref, m_ref, l_ref, ls_ref, *, tq, tk, lam_init):
    i = pl.program_id(2)
    dh = DIFF_HEAD_DIM
    ks, qs = ATTN_KS, ATTN_QS
    nkb = tq // tk
    n_strips, n_sub = 2 * tq // qs, tk // ks

    qt = qt_ref[...].astype(F32) * (dh ** -0.5 * LOG2_E)
    row = lax.broadcasted_iota(jnp.int32, qt.shape, 0)
    qz = jnp.concatenate([jnp.where(row < dh, qt, 0.0),
                          jnp.where(row >= dh, qt, 0.0)], axis=1).astype(BF16)

    def init():
        m_ref[...] = jnp.full(m_ref.shape, NEG_BIG, F32)
        l_ref[...] = jnp.zeros_like(l_ref)
        acc_ref[...] = jnp.zeros_like(acc_ref)
        ls_ref[...] = jnp.zeros_like(ls_ref)

    def chunk_mask(shape, k_lo, q_lo):
        kc = (k_lo + lax.broadcasted_iota(jnp.int32, shape, 0)) // CHUNK
        qc = (q_lo + lax.broadcasted_iota(jnp.int32, shape, 1)) // CHUNK
        return kc <= qc

    def exact_step(j, koff):
        k0 = pl.multiple_of(j * tk, tk)
        kb = k_ref[pl.ds(k0, tk), :]
        s = jnp.dot(kb, qz, preferred_element_type=F32)
        if koff is not None:
            keep = chunk_mask((tk, tq), koff, 0)
            s = jnp.where(jnp.concatenate([keep, keep], axis=1), s, NEG_BIG)
        m_old = m_ref[...]
        m_new = jnp.maximum(m_old, jnp.max(s, axis=0, keepdims=True))
        corr = jnp.exp2(m_old - m_new)
        p = jnp.exp2(s - m_new)
        l_ref[...] = corr * l_ref[...] + jnp.sum(p, axis=0, keepdims=True)
        vtb = vt_ref[:, pl.ds(k0, tk)]
        acc_ref[...] = corr * acc_ref[...] + jnp.dot(vtb, p.astype(BF16),
                                                     preferred_element_type=F32)
        m_ref[...] = m_new

    def scores(j, koff, c, r):
        q_lo = (c * qs) % tq
        kb = k_ref[pl.ds(pl.multiple_of(j * tk, tk) + r * ks, ks), :]
        s = jnp.dot(kb, qz[:, c * qs:(c + 1) * qs], preferred_element_type=F32)
        if koff is not None and (koff + r * ks + ks - 1) // CHUNK > q_lo // CHUNK:
            s = jnp.where(chunk_mask(s.shape, koff + r * ks, q_lo), s, NEG_BIG)
        return s

    def prologue():
        init()
        s = jnp.dot(k_ref[0:CHUNK, :], qz, preferred_element_type=F32)
        m_ref[...] = jnp.max(s, axis=0, keepdims=True)

    def fast_steps(blocks):
        def above_diagonal(c, b, r):
            koff = blocks[b][1]
            return koff is not None and (koff + r * ks) // CHUNK > ((c * qs) % tq + qs - 1) // CHUNK

        tiles = [(c, b, r) for c in range(n_strips) for b in range(len(blocks)) for r in range(n_sub)
                 if not above_diagonal(c, b, r)]
        score = lambda c, b, r: scores(blocks[b][0], blocks[b][1], c, r)
        in_flight = [score(*t) for t in tiles[:ATTN_LOOKAHEAD]]
        for n, (c, b, r) in enumerate(tiles):
            lanes = slice(c * qs, (c + 1) * qs)
            s = in_flight.pop(0)
            if n + ATTN_LOOKAHEAD < len(tiles):
                in_flight.append(score(*tiles[n + ATTN_LOOKAHEAD]))
            if n == 0 or tiles[n - 1][0] != c:
                shift = m_ref[:, lanes]
                lsum = ls_ref[:, lanes]
                pv = None
            e = jnp.exp2(s - shift)
            lsum = lsum + jnp.sum(e.reshape(ks // F32_SUBLANES, F32_SUBLANES, qs), axis=0)
            vtb = vt_ref[:, pl.ds(pl.multiple_of(blocks[b][0] * tk, tk) + r * ks, ks)]
            d = jnp.dot(vtb, e.astype(BF16), preferred_element_type=F32)
            pv = d if pv is None else pv + d
            if n + 1 == len(tiles) or tiles[n + 1][0] != c:
                acc_ref[:, lanes] += pv
                ls_ref[:, lanes] = lsum

    diagonal = [(nkb * i + d, d * tk) for d in range(nkb)]

    def exact_body(j, carry):
        exact_step(j, None)
        return carry

    def fast_body(p, carry):
        fast_steps([(2 * nkb * p + d, None) for d in range(2 * nkb)])
        return carry

    prologue()

    @pl.when(i > 0)
    def _():
        lax.fori_loop(0, i // 2, fast_body, 0)

        @pl.when(i % 2 == 1)
        def _():
            fast_steps([(nkb * (i - 1) + d, None) for d in range(nkb)])

    fast_steps(diagonal)
    in_range = jnp.max(ls_ref[...]) <= 2.0 ** ATTN_SHIFT_SLACK

    @pl.when(jnp.logical_not(in_range))
    def _():
        init()
        lax.fori_loop(0, nkb * i, exact_body, 0)
        for j, koff in diagonal:
            exact_step(j, koff)

    lam_p = lam_ref[...]
    lam = (jnp.exp(jnp.sum(lam_p[0:1] * lam_p[1:2], axis=-1, keepdims=True))
           - jnp.exp(jnp.sum(lam_p[2:3] * lam_p[3:4], axis=-1, keepdims=True))
           + lam_init)
    acc = acc_ref[...]
    l = l_ref[...] + jnp.sum(ls_ref[...], axis=0, keepdims=True)
    ot = acc[:, :tq] / l[:, :tq] - lam * (acc[:, tq:] / l[:, tq:])
    ot = ot * lax.rsqrt(jnp.mean(ot * ot, axis=0, keepdims=True) + RMS_EPS)
    o_ref[...] = (ot * g_ref[...] * (1.0 - lam_init)).T.astype(o_ref.dtype)


def _diff_attn(qvt, ku, lam_p, g, lam_init, batch, seq):
    tk = min(ATTN_TK, seq)
    tq = min(ATTN_TQ, seq)
    assert tq % tk == 0 and tk % CHUNK == 0 and seq % tq == 0
    nq = seq // tq
    hh = DIFF_HEADS
    dv = DIFF_V_DIM
    kern = functools.partial(_diff_attn_kernel, tq=tq, tk=tk, lam_init=lam_init)
    return pl.pallas_call(
        kern,
        grid=(batch, hh, nq),
        in_specs=[
            pl.BlockSpec((dv, tq), lambda b, h, i: (h, b * nq + i)),
            pl.BlockSpec((seq, dv), lambda b, h, i: (b, h)),
            pl.BlockSpec((dv, seq), lambda b, h, i: (hh + h, b)),
            pl.BlockSpec((4, DIFF_HEAD_DIM), lambda b, h, i: (0, 0)),
            pl.BlockSpec((dv, 1), lambda b, h, i: (0, 0)),
        ],
        out_specs=pl.BlockSpec((tq, dv), lambda b, h, i: (b * nq + i, h)),
        out_shape=jax.ShapeDtypeStruct((batch * seq, hh * dv), BF16),
        scratch_shapes=[pltpu.VMEM((dv, 2 * tq), F32),
                        pltpu.VMEM((1, 2 * tq), F32),
                        pltpu.VMEM((1, 2 * tq), F32),
                        pltpu.VMEM((F32_SUBLANES, 2 * tq), F32)],
        compiler_params=_params("parallel", "parallel", "arbitrary"),
        name="diff_attn",
    )(qvt, ku, qvt, lam_p, g)


def _s5_kernel(ut_ref, taps_ref, gt_ref, ct_ref, apow_ref, y_ref, mt_ref, *, batch):
    L, P, N = S5_CHUNK, SSM_GROUP, SSM_STATE
    ut = ut_ref[0]
    cols = ut.shape[1]
    per_b = cols // batch

    x = jnp.dot(gt_ref[0], ut, preferred_element_type=F32)
    x_re, x_im = x[:N], x[N:]
    chunk = lax.broadcasted_iota(jnp.int32, (N, cols), 1) % per_b
    for k in range(apow_ref.shape[1]):
        sh = 1 << k
        w_re, w_im = apow_ref[0, k, :, 0:1], apow_ref[0, k, :, 1:2]
        r_re = jnp.where(chunk >= sh, pltpu.roll(x_re, sh, 1), 0.0)
        r_im = jnp.where(chunk >= sh, pltpu.roll(x_im, sh, 1), 0.0)
        x_re, x_im = x_re + (w_re * r_re - w_im * r_im), x_im + (w_re * r_im + w_im * r_re)
    st = jnp.concatenate([jnp.where(chunk >= 1, pltpu.roll(x_re, 1, 1), 0.0),
                          jnp.where(chunk >= 1, pltpu.roll(x_im, 1, 1), 0.0)], axis=0)
    hi = st.astype(BF16)
    lo = (st - hi.astype(F32)).astype(BF16)

    keep = (lax.broadcasted_iota(jnp.int32, (L, P * L), 1) % L
            <= lax.broadcasted_iota(jnp.int32, (L, P * L), 0))
    for p in range(P):
        base = jnp.broadcast_to(taps_ref[0, p:p + 1, :], (L, P * L))
        rows = pltpu.roll(base, P * L - (L - 1), 1, stride=1, stride_axis=0)
        mt_ref[p * L:(p + 1) * L, :] = jnp.where(keep, rows, 0.0).astype(BF16)

    ct = ct_ref[0]
    y = (jnp.dot(mt_ref[...], ut, preferred_element_type=F32)
         + jnp.dot(ct, hi, preferred_element_type=F32)
         + jnp.dot(ct, lo, preferred_element_type=F32))
    y_ref[0] = jax.nn.gelu(y).astype(y_ref.dtype)


def _s5_conv(ut_g, taps, g_t, c_t, a_pow, layer, batch):
    groups, lp, cols = ut_g.shape
    n = SSM_STATE
    kern = functools.partial(_s5_kernel, batch=batch)
    per_group = lambda *shape: pl.BlockSpec((1,) + shape, lambda g: (g,) + (0,) * len(shape))
    of_layer = lambda *shape: pl.BlockSpec((None, 1) + shape, lambda g: (layer, g) + (0,) * len(shape))
    return pl.pallas_call(
        kern,
        grid=(groups,),
        in_specs=[per_group(lp, cols), of_layer(SSM_GROUP, lp), of_layer(2 * n, lp),
                  of_layer(lp, 2 * n), of_layer(*a_pow.shape[2:])],
        out_specs=per_group(lp, cols),
        out_shape=jax.ShapeDtypeStruct((groups, lp, cols), BF16),
        scratch_shapes=[pltpu.VMEM((lp, lp), BF16)],
        compiler_params=_params("parallel"),
        name="s5_conv",
    )(ut_g, taps, g_t, c_t, a_pow)


def _s5_operators(lam_re, lam_im, log_step, b_re, b_im, c_re, c_im, d_skip, n_chunks):
    hp = lax.Precision.HIGHEST
    L, P, G, N = S5_CHUNK, SSM_GROUP, SSM_GROUPS, SSM_STATE
    lr = jnp.minimum(lam_re.astype(F32), -1e-4)
    li = lam_im.astype(F32)
    step = jnp.exp(log_step.astype(F32))[:, None]
    dr, di = lr * step, li * step
    tau = jnp.arange(L + 1, dtype=F32)
    mag = jnp.exp(dr[..., None] * tau)
    pw_re, pw_im = mag * jnp.cos(di[..., None] * tau), mag * jnp.sin(di[..., None] * tau)
    a_re, a_im = pw_re[..., 1], pw_im[..., 1]
    den = lr * lr + li * li
    f_re = ((a_re - 1.0) * lr + a_im * li) / den
    f_im = (a_im * lr - (a_re - 1.0) * li) / den
    br, bi = b_re.astype(F32), b_im.astype(F32)
    bb_re = f_re[..., None] * br - f_im[..., None] * bi
    bb_im = f_re[..., None] * bi + f_im[..., None] * br
    cr, ci = c_re.astype(F32), c_im.astype(F32)
    col = jnp.arange(P * L)
    pick_tau = (jnp.arange(L)[:, None] == L - 1 - col[None, :] % L).astype(F32)
    pick_p = (jnp.arange(P)[:, None] == col[None, :] // L).astype(F32)
    rv_re = jnp.einsum('gnt,tx->gnx', pw_re[..., :L], pick_tau, precision=hp)
    rv_im = jnp.einsum('gnt,tx->gnx', pw_im[..., :L], pick_tau, precision=hp)
    bx_re = jnp.einsum('gnp,px->gnx', bb_re, pick_p, precision=hp)
    bx_im = jnp.einsum('gnp,px->gnx', bb_im, pick_p, precision=hp)
    g_re, g_im = rv_re * bx_re - rv_im * bx_im, rv_re * bx_im + rv_im * bx_re
    g_t = jnp.concatenate([g_re, g_im], axis=1)
    taps = (jnp.einsum('gpn,gnx->gpx', cr, g_re, precision=hp)
            - jnp.einsum('gpn,gnx->gpx', ci, g_im, precision=hp))
    tap0 = (col[None, :] == jnp.arange(P)[:, None] * L + L - 1).astype(F32)
    taps = taps + d_skip.astype(F32).reshape(G, P, 1) * tap0
    cr, ci = cr[:, :, None, :], ci[:, :, None, :]
    nx_re = pw_re[..., 1:].transpose(0, 2, 1)[:, None]
    nx_im = pw_im[..., 1:].transpose(0, 2, 1)[:, None]
    c_t = jnp.concatenate([cr * nx_re - ci * nx_im, -(cr * nx_im + ci * nx_re)],
                          axis=-1).reshape(G, P * L, 2 * N)
    n_steps = max(1, (n_chunks - 1).bit_length())
    span = (L * 2.0 ** jnp.arange(n_steps, dtype=F32))[None, :, None]
    mag = jnp.exp(span * dr[:, None, :])
    a_pow = jnp.stack([mag * jnp.cos(span * di[:, None, :]), mag * jnp.sin(span * di[:, None, :])], axis=-1)
    return taps, g_t.astype(BF16), c_t.astype(BF16), a_pow


def _mix_out_kernel(x_ref, attn_ref, y_ref, gw_ref, gb_ref, wo_ref, g_ref, b_ref, o_ref):
    aw = attn_ref.shape[1]
    slab = x_ref.shape[0] // EPILOGUE_SLABS
    for r in range(EPILOGUE_SLABS):
        rows = slice(r * slab, (r + 1) * slab)
        yb = y_ref[rows, :]
        z = jnp.dot(yb, gw_ref[...], preferred_element_type=F32) + gb_ref[...]
        y = yb.astype(F32) * jax.nn.sigmoid(z)
        mixed = (jnp.dot(attn_ref[rows, :], wo_ref[:aw, :], preferred_element_type=F32)
                 + jnp.dot(y.astype(BF16), wo_ref[aw:, :], preferred_element_type=F32))
        o_ref[rows, :] = _layer_norm(ALPHA * x_ref[rows, :] + mixed, g_ref[...], b_ref[...])


def _mix_out(x, attn, y_ssm, glu_w, glu_b, w_out, g, b):
    t, d = x.shape
    tm = min(PROJ_TM, t)
    aw, sw = attn.shape[1], y_ssm.shape[1]
    row = lambda i: (i, 0)
    return pl.pallas_call(
        _mix_out_kernel,
        grid=(t // tm,),
        in_specs=[
            pl.BlockSpec((tm, d), row),
            pl.BlockSpec((tm, aw), row),
            pl.BlockSpec((tm, sw), row),
            _resident((sw, sw)),
            _resident((1, sw)),
            _resident((d, d)),
            _resident((1, d)),
            _resident((1, d)),
        ],
        out_specs=pl.BlockSpec((tm, d), row),
        out_shape=jax.ShapeDtypeStruct((t, d), F32),
        compiler_params=_params("parallel"),
        name="mix_out",
    )(x, attn, y_ssm, glu_w, glu_b, w_out, g, b)


def _mem_fold_kernel(m_ref, wq_ref, wk_ref, wv_ref, wo_ref, qk_ref, vo_ref):
    mb = m_ref[0].astype(BF16)
    k = jnp.dot(mb, wk_ref[...], preferred_element_type=F32).astype(BF16)
    v = jnp.dot(mb, wv_ref[...], preferred_element_type=F32).astype(BF16)
    qk = lax.dot_general(wq_ref[...], k, (((1,), (1,)), ((), ())),
                         preferred_element_type=F32)
    qk_ref[0] = (qk * (XATTN_HEAD_DIM ** -0.5)).astype(BF16)
    vo_ref[0] = jnp.dot(v, wo_ref[...], preferred_element_type=F32).astype(BF16)


def _mem_fold(mem, wq, wk, wv, wo):
    batch, n_mem, d = mem.shape
    hd = XATTN_HEAD_DIM
    col = lambda b, h: (0, h)
    return pl.pallas_call(
        _mem_fold_kernel,
        grid=(batch, XATTN_HEADS),
        in_specs=[pl.BlockSpec((1, n_mem, d), lambda b, h: (b, 0, 0)),
                  pl.BlockSpec((d, hd), col),
                  pl.BlockSpec((d, hd), col),
                  pl.BlockSpec((d, hd), col),
                  pl.BlockSpec((hd, d), lambda b, h: (h, 0))],
        out_specs=[pl.BlockSpec((1, d, n_mem), lambda b, h: (b, 0, h)),
                   pl.BlockSpec((1, n_mem, d), lambda b, h: (b, h, 0))],
        out_shape=[jax.ShapeDtypeStruct((batch, d, XATTN_HEADS * n_mem), BF16),
                   jax.ShapeDtypeStruct((batch, XATTN_HEADS * n_mem, d), BF16)],
        compiler_params=_params("parallel", "parallel"),
        name="mem_fold",
    )(mem, wq, wk, wv, wo)


def _xattn_ln_kernel(x_ref, qk_ref, vo_ref, g_ref, b_ref, o_ref, *, n_mem):
    x = x_ref[...]
    s = jnp.dot(x.astype(BF16), qk_ref[0], preferred_element_type=F32)
    probs = []
    for h in range(XATTN_HEADS):
        sh = s[:, h * n_mem:(h + 1) * n_mem]
        e = jnp.exp(sh - jnp.max(sh, axis=-1, keepdims=True))
        probs.append((e / jnp.sum(e, axis=-1, keepdims=True)).astype(BF16))
    out = jnp.dot(jnp.concatenate(probs, axis=1), vo_ref[0], preferred_element_type=F32)
    o_ref[...] = _layer_norm(ALPHA * x + out, g_ref[...], b_ref[...])


def _xattn_ln(x, qk, vo, g, b, seq):
    t, d = x.shape
    tm = min(PROJ_TM, seq)
    per_b = seq // tm
    hm = qk.shape[2]
    kern = functools.partial(_xattn_ln_kernel, n_mem=hm // XATTN_HEADS)
    return pl.pallas_call(
        kern,
        grid=(t // tm,),
        in_specs=[
            pl.BlockSpec((tm, d), lambda i: (i, 0)),
            pl.BlockSpec((1, d, hm), lambda i: (i // per_b, 0, 0)),
            pl.BlockSpec((1, hm, d), lambda i: (i // per_b, 0, 0)),
            _resident((1, d)),
            _resident((1, d)),
        ],
        out_specs=pl.BlockSpec((tm, d), lambda i: (i, 0)),
        out_shape=jax.ShapeDtypeStruct((t, d), F32),
        compiler_params=_params("parallel"),
        name="xattn_ln",
    )(x, qk, vo, g, b)


def kernel(x, mem, ffn1_w_gate, ffn1_w_up, ffn1_w_down, ln1_g, ln1_b, w_in, lambda_q1, lambda_k1, lambda_q2, lambda_k2, diff_norm_g, ssm_lambda_re, ssm_lambda_im, ssm_log_step, ssm_b_re, ssm_b_im, ssm_c_re, ssm_c_im, ssm_d, ssm_glu_w, ssm_glu_b, w_out, ln2_g, ln2_b, xattn_w_q, xattn_w_k, xattn_w_v, xattn_w_o, ln3_g, ln3_b, ffn2_w_gate, ffn2_w_up, ffn2_w_down, ln4_g, ln4_b):
    batch, seq, d = x.shape
    t = batch * seq
    aw = ATTN_WIDTH
    L, P, G = S5_CHUNK, SSM_GROUP, SSM_GROUPS
    n_chunks = seq // L
    xf = x.reshape(t, d)
    row = lambda a: a.reshape(1, -1).astype(F32)
    s5_ops = jax.vmap(functools.partial(_s5_operators, n_chunks=n_chunks))(
        ssm_lambda_re, ssm_lambda_im, ssm_log_step, ssm_b_re, ssm_b_im, ssm_c_re, ssm_c_im, ssm_d)

    for l in range(DEPTH):
        lam_init = 0.8 - 0.6 * math.exp(-0.3 * l)
        xf = _ffn_ln(xf, ffn1_w_gate[l].astype(BF16), ffn1_w_up[l].astype(BF16),
                     ffn1_w_down[l].astype(BF16), row(ln1_g[l]), row(ln1_b[l]))

        k_nat, qvut = _in_proj(xf, w_in[l].astype(BF16), w_in[l].T.astype(BF16))

        lam_p = jnp.stack([lambda_q1[l], lambda_k1[l], lambda_q2[l], lambda_k2[l]]).astype(F32)
        attn = _diff_attn(qvut, k_nat, lam_p, diff_norm_g[l].reshape(-1, 1).astype(F32), lam_init, batch, seq)

        ut_g = (qvut[2 * aw:].reshape(G * P, batch * n_chunks, L)
                .swapaxes(1, 2).reshape(G, P * L, batch * n_chunks))
        yt_g = _s5_conv(ut_g, *s5_ops, l, batch)
        y_ssm = yt_g.reshape(G * P, L, batch * n_chunks).transpose(2, 1, 0).reshape(t, G * P)

        xf = _mix_out(xf, attn, y_ssm, ssm_glu_w[l].astype(BF16),
                      row(ssm_glu_b[l]), w_out[l].astype(BF16), row(ln2_g[l]), row(ln2_b[l]))

        qk, vo = _mem_fold(mem, xattn_w_q[l].astype(BF16), xattn_w_k[l].astype(BF16),
                           xattn_w_v[l].astype(BF16), xattn_w_o[l].astype(BF16))
        xf = _xattn_ln(xf, qk, vo, row(ln3_g[l]), row(ln3_b[l]), seq)

        xf = _ffn_ln(xf, ffn2_w_gate[l].astype(BF16), ffn2_w_up[l].astype(BF16),
                     ffn2_w_down[l].astype(BF16), row(ln4_g[l]), row(ln4_b[l]))
    return xf.reshape(batch, seq, d)
```

```python
import functools
import math

import jax
import jax.numpy as jnp
from jax import lax
from jax.experimental import pallas as pl
from jax.experimental.pallas import tpu as pltpu

F32 = jnp.float32
BF16 = jnp.bfloat16

D_MODEL = 2048
DEPTH = 4
CHUNK = 64
ATTN_WIDTH = D_MODEL // 2
SSM_WIDTH = D_MODEL - ATTN_WIDTH
DIFF_HEAD_DIM = 64
DIFF_V_DIM = 2 * DIFF_HEAD_DIM
DIFF_HEADS = ATTN_WIDTH // DIFF_V_DIM
SSM_GROUP = 16
SSM_GROUPS = SSM_WIDTH // SSM_GROUP
SSM_STATE = 64
XATTN_HEADS = 4
XATTN_HEAD_DIM = D_MODEL // XATTN_HEADS
ALPHA = (2 * DEPTH) ** 0.25
LN_EPS = 1e-5
RMS_EPS = 1e-5
NEG_BIG = -1e30
LOG2_E = 1.4426950408889634

V7X_VMEM_LIMIT_BYTES = 56 * 1024 * 1024
F32_SUBLANES = 8

FFN_TM = 512
FFN_TF = 512
FFN_STEP_CHUNKS = 2
PROJ_TM = 512
EPILOGUE_SLABS = 2
ATTN_TQ = 1024
ATTN_TK = 512
ATTN_KS = 256
ATTN_QS = 512
ATTN_LOOKAHEAD = 2
ATTN_SHIFT_SLACK = 64.0
S5_CHUNK = 64


def _params(*sem):
    return pltpu.CompilerParams(dimension_semantics=sem,
                                vmem_limit_bytes=V7X_VMEM_LIMIT_BYTES)


def _resident(shape):
    return pl.BlockSpec(shape, lambda *_: (0,) * len(shape), pipeline_mode=pl.Buffered(1))


def _layer_norm(y, g, b):
    mu = jnp.mean(y, axis=-1, keepdims=True)
    yc = y - mu
    var = jnp.mean(yc * yc, axis=-1, keepdims=True)
    return yc * lax.rsqrt(var + LN_EPS) * g + b


def _ffn_ln_kernel(x_ref, wg_ref, wu_ref, wd_ref, g_ref, b_ref, o_ref, xb_ref, acc_ref, *, last_chunks):
    f = pl.program_id(1)
    last = pl.num_programs(1) - 1
    tf = FFN_TF

    def chunks(n, xb=None):
        xb = xb_ref[...] if xb is None else xb
        total = None
        for c in range(n):
            cols = slice(c * tf, (c + 1) * tf)
            gate = jnp.dot(xb, wg_ref[:, cols], preferred_element_type=F32)
            up = jnp.dot(xb, wu_ref[:, cols], preferred_element_type=F32)
            h = (gate * jax.nn.sigmoid(gate) * up).astype(BF16)
            part = jnp.dot(h, wd_ref[cols, :], preferred_element_type=F32)
            total = part if total is None else total + part
        return total

    @pl.when(f == 0)
    def _():
        xb = x_ref[...].astype(BF16)
        xb_ref[...] = xb
        acc_ref[...] = chunks(FFN_STEP_CHUNKS, xb)

    @pl.when((f > 0) & (f < last))
    def _():
        acc_ref[...] += chunks(FFN_STEP_CHUNKS)

    @pl.when(f == last)
    def _():
        y = ALPHA * x_ref[...] + 0.5 * (acc_ref[...] + chunks(last_chunks))
        o_ref[...] = _layer_norm(y, g_ref[...], b_ref[...])


def _ffn_ln(x, wg, wu, wd, g, b):
    t, d = x.shape
    d_ff = wg.shape[1]
    tm, ts = min(FFN_TM, t), FFN_TF * FFN_STEP_CHUNKS
    steps = pl.cdiv(d_ff, ts)
    assert steps >= 2 and d_ff % FFN_TF == 0
    last_chunks = d_ff // FFN_TF - (steps - 1) * FFN_STEP_CHUNKS
    kern = functools.partial(_ffn_ln_kernel, last_chunks=last_chunks)
    return pl.pallas_call(
        kern,
        grid=(t // tm, steps),
        in_specs=[
            pl.BlockSpec((tm, d), lambda i, j: (i, 0)),
            pl.BlockSpec((d, ts), lambda i, j: (0, j)),
            pl.BlockSpec((d, ts), lambda i, j: (0, j)),
            pl.BlockSpec((ts, d), lambda i, j: (j, 0)),
            pl.BlockSpec((1, d), lambda i, j: (0, 0)),
            pl.BlockSpec((1, d), lambda i, j: (0, 0)),
        ],
        out_specs=pl.BlockSpec((tm, d), lambda i, j: (i, 0)),
        out_shape=jax.ShapeDtypeStruct((t, d), F32),
        scratch_shapes=[pltpu.VMEM((tm, d), BF16), pltpu.VMEM((tm, d), F32)],
        compiler_params=_params("parallel", "arbitrary"),
        name="ffn_ln",
    )(x, wg, wu, wd, g, b)


def _in_proj_kernel(x_ref, wk_ref, wqt_ref, wvut_ref, k_ref, qvut_ref):
    xb = x_ref[...].astype(BF16)
    k_ref[...] = jnp.dot(xb, wk_ref[...], preferred_element_type=F32).astype(BF16)
    nt = (((1,), (1,)), ((), ()))
    nq = wqt_ref.shape[0]
    qvut_ref[:nq, :] = lax.dot_general(wqt_ref[...], xb, nt, preferred_element_type=F32).astype(BF16)
    qvut_ref[nq:, :] = lax.dot_general(wvut_ref[...], xb, nt, preferred_element_type=F32).astype(BF16)


def _in_proj(x, w, w_t):
    t, d = x.shape
    tm = min(PROJ_TM, t)
    aw = ATTN_WIDTH
    n_t = w.shape[1] - aw
    once = pl.Buffered(1)
    return pl.pallas_call(
        _in_proj_kernel,
        grid=(t // tm,),
        in_specs=[
            pl.BlockSpec((tm, d), lambda i: (i, 0)),
            pl.BlockSpec((d, aw), lambda i: (0, 1), pipeline_mode=once),
            pl.BlockSpec((aw, d), lambda i: (0, 0), pipeline_mode=once),
            pl.BlockSpec((2 * aw, d), lambda i: (1, 0), pipeline_mode=once),
        ],
        out_specs=[
            pl.BlockSpec((tm, aw), lambda i: (i, 0)),
            pl.BlockSpec((n_t, tm), lambda i: (0, i)),
        ],
        out_shape=[jax.ShapeDtypeStruct((t, aw), BF16),
                   jax.ShapeDtypeStruct((n_t, t), BF16)],
        compiler_params=_params("parallel"),
        name="in_proj",
    )(x, w, w_t, w_t)


def _diff_attn_kernel(qt_ref, k_ref, vt_ref, lam_ref, g_ref, o_ref,
                      acc_ref, m_ref, l_ref, ls_ref, *, tq, tk, lam_init):
    i = pl.program_id(2)
    dh = DIFF_HEAD_DIM
    ks, qs = ATTN_KS, ATTN_QS
    nkb = tq // tk
    n_strips, n_sub = 2 * tq // qs, tk // ks

    qt = qt_ref[...].astype(F32) * (dh ** -0.5 * LOG2_E)
    row = lax.broadcasted_iota(jnp.int32, qt.shape, 0)
    qz = jnp.concatenate([jnp.where(row < dh, qt, 0.0),
                          jnp.where(row >= dh, qt, 0.0)], axis=1).astype(BF16)

    def init():
        m_ref[...] = jnp.full(m_ref.shape, NEG_BIG, F32)
        l_ref[...] = jnp.zeros_like(l_ref)
        acc_ref[...] = jnp.zeros_like(acc_ref)
        ls_ref[...] = jnp.zeros_like(ls_ref)

    def chunk_mask(shape, k_lo, q_lo):
        kc = (k_lo + lax.broadcasted_iota(jnp.int32, shape, 0)) // CHUNK
        qc = (q_lo + lax.broadcasted_iota(jnp.int32, shape, 1)) // CHUNK
        return kc <= qc

    def exact_step(j, koff):
        k0 = pl.multiple_of(j * tk, tk)
        kb = k_ref[pl.ds(k0, tk), :]
        s = jnp.dot(kb, qz, preferred_element_type=F32)
        if koff is not None:
            keep = chunk_mask((tk, tq), koff, 0)
            s = jnp.where(jnp.concatenate([keep, keep], axis=1), s, NEG_BIG)
        m_old = m_ref[...]
        m_new = jnp.maximum(m_old, jnp.max(s, axis=0, keepdims=True))
        corr = jnp.exp2(m_old - m_new)
        p = jnp.exp2(s - m_new)
        l_ref[...] = corr * l_ref[...] + jnp.sum(p, axis=0, keepdims=True)
        vtb = vt_ref[:, pl.ds(k0, tk)]
        acc_ref[...] = corr * acc_ref[...] + jnp.dot(vtb, p.astype(BF16),
                                                     preferred_element_type=F32)
        m_ref[...] = m_new

    def scores(j, koff, c, r):
        q_lo = (c * qs) % tq
        kb = k_ref[pl.ds(pl.multiple_of(j * tk, tk) + r * ks, ks), :]
        s = jnp.dot(kb, qz[:, c * qs:(c + 1) * qs], preferred_element_type=F32)
        if koff is not None and (koff + r * ks + ks - 1) // CHUNK > q_lo // CHUNK:
            s = jnp.where(chunk_mask(s.shape, koff + r * ks, q_lo), s, NEG_BIG)
        return s

    def prologue():
        init()
        s = jnp.dot(k_ref[0:CHUNK, :], qz, preferred_element_type=F32)
        m_ref[...] = jnp.max(s, axis=0, keepdims=True)

    def fast_steps(blocks):
        def above_diagonal(c, b, r):
            koff = blocks[b][1]
            return koff is not None and (koff + r * ks) // CHUNK > ((c * qs) % tq + qs - 1) // CHUNK

        tiles = [(c, b, r) for c in range(n_strips) for b in range(len(blocks)) for r in range(n_sub)
                 if not above_diagonal(c, b, r)]
        score = lambda c, b, r: scores(blocks[b][0], blocks[b][1], c, r)
        in_flight = [score(*t) for t in tiles[:ATTN_LOOKAHEAD]]
        for n, (c, b, r) in enumerate(tiles):
            lanes = slice(c * qs, (c + 1) * qs)
            s = in_flight.pop(0)
            if n + ATTN_LOOKAHEAD < len(tiles):
                in_flight.append(score(*tiles[n + ATTN_LOOKAHEAD]))
            if n == 0 or tiles[n - 1][0] != c:
                shift = m_ref[:, lanes]
                lsum = ls_ref[:, lanes]
                pv = None
            e = jnp.exp2(s - shift)
            lsum = lsum + jnp.sum(e.reshape(ks // F32_SUBLANES, F32_SUBLANES, qs), axis=0)
            vtb = vt_ref[:, pl.ds(pl.multiple_of(blocks[b][0] * tk, tk) + r * ks, ks)]
            d = jnp.dot(vtb, e.astype(BF16), preferred_element_type=F32)
            pv = d if pv is None else pv + d
            if n + 1 == len(tiles) or tiles[n + 1][0] != c:
                acc_ref[:, lanes] += pv
                ls_ref[:, lanes] = lsum

    diagonal = [(nkb * i + d, d * tk) for d in range(nkb)]

    def exact_body(j, carry):
        exact_step(j, None)
        return carry

    def fast_body(p, carry):
        fast_steps([(2 * nkb * p + d, None) for d in range(2 * nkb)])
        return carry

    prologue()

    @pl.when(i > 0)
    def _():
        lax.fori_loop(0, i // 2, fast_body, 0)

        @pl.when(i % 2 == 1)
        def _():
            fast_steps([(nkb * (i - 1) + d, None) for d in range(nkb)])

    fast_steps(diagonal)
    in_range = jnp.max(ls_ref[...]) <= 2.0 ** ATTN_SHIFT_SLACK

    @pl.when(jnp.logical_not(in_range))
    def _():
        init()
        lax.fori_loop(0, nkb * i, exact_body, 0)
        for j, koff in diagonal:
            exact_step(j, koff)

    lam_p = lam_ref[...]
    lam = (jnp.exp(jnp.sum(lam_p[0:1] * lam_p[1:2], axis=-1, keepdims=True))
           - jnp.exp(jnp.sum(lam_p[2:3] * lam_p[3:4], axis=-1, keepdims=True))
           + lam_init)
    acc = acc_ref[...]
    l = l_ref[...] + jnp.sum(ls_ref[...], axis=0, keepdims=True)
    ot = acc[:, :tq] / l[:, :tq] - lam * (acc[:, tq:] / l[:, tq:])
    ot = ot * lax.rsqrt(jnp.mean(ot * ot, axis=0, keepdims=True) + RMS_EPS)
    o_ref[...] = (ot * g_ref[...] * (1.0 - lam_init)).T.astype(o_ref.dtype)


def _diff_attn(qvt, ku, lam_p, g, lam_init, batch, seq):
    tk = min(ATTN_TK, seq)
    tq = min(ATTN_TQ, seq)
    assert tq % tk == 0 and tk % CHUNK == 0 and seq % tq == 0
    nq = seq // tq
    hh = DIFF_HEADS
    dv = DIFF_V_DIM
    kern = functools.partial(_diff_attn_kernel, tq=tq, tk=tk, lam_init=lam_init)
    return pl.pallas_call(
        kern,
        grid=(batch, hh, nq),
        in_specs=[
            pl.BlockSpec((dv, tq), lambda b, h, i: (h, b * nq + i)),
            pl.BlockSpec((seq, dv), lambda b, h, i: (b, h)),
            pl.BlockSpec((dv, seq), lambda b, h, i: (hh + h, b)),
            pl.BlockSpec((4, DIFF_HEAD_DIM), lambda b, h, i: (0, 0)),
            pl.BlockSpec((dv, 1), lambda b, h, i: (0, 0)),
        ],
        out_specs=pl.BlockSpec((tq, dv), lambda b, h, i: (b * nq + i, h)),
        out_shape=jax.ShapeDtypeStruct((batch * seq, hh * dv), BF16),
        scratch_shapes=[pltpu.VMEM((dv, 2 * tq), F32),
                        pltpu.VMEM((1, 2 * tq), F32),
                        pltpu.VMEM((1, 2 * tq), F32),
                        pltpu.VMEM((F32_SUBLANES, 2 * tq), F32)],
        compiler_params=_params("parallel", "parallel", "arbitrary"),
        name="diff_attn",
    )(qvt, ku, qvt, lam_p, g)


def _s5_kernel(ut_ref, taps_ref, gt_ref, ct_ref, apow_ref, y_ref, mt_ref, *, batch):
    L, P, N = S5_CHUNK, SSM_GROUP, SSM_STATE
    ut = ut_ref[0]
    cols = ut.shape[1]
    per_b = cols // batch

    x = jnp.dot(gt_ref[0], ut, preferred_element_type=F32)
    x_re, x_im = x[:N], x[N:]
    chunk = lax.broadcasted_iota(jnp.int32, (N, cols), 1) % per_b
    for k in range(apow_ref.shape[1]):
        sh = 1 << k
        w_re, w_im = apow_ref[0, k, :, 0:1], apow_ref[0, k, :, 1:2]
        r_re = jnp.where(chunk >= sh, pltpu.roll(x_re, sh, 1), 0.0)
        r_im = jnp.where(chunk >= sh, pltpu.roll(x_im, sh, 1), 0.0)
        x_re, x_im = x_re + (w_re * r_re - w_im * r_im), x_im + (w_re * r_im + w_im * r_re)
    st = jnp.concatenate([jnp.where(chunk >= 1, pltpu.roll(x_re, 1, 1), 0.0),
                          jnp.where(chunk >= 1, pltpu.roll(x_im, 1, 1), 0.0)], axis=0)
    hi = st.astype(BF16)
    lo = (st - hi.astype(F32)).astype(BF16)

    keep = (lax.broadcasted_iota(jnp.int32, (L, P * L), 1) % L
            <= lax.broadcasted_iota(jnp.int32, (L, P * L), 0))
    for p in range(P):
        base = jnp.broadcast_to(taps_ref[0, p:p + 1, :], (L, P * L))
        rows = pltpu.roll(base, P * L - (L - 1), 1, stride=1, stride_axis=0)
        mt_ref[p * L:(p + 1) * L, :] = jnp.where(keep, rows, 0.0).astype(BF16)

    ct = ct_ref[0]
    y = (jnp.dot(mt_ref[...], ut, preferred_element_type=F32)
         + jnp.dot(ct, hi, preferred_element_type=F32)
         + jnp.dot(ct, lo, preferred_element_type=F32))
    y_ref[0] = jax.nn.gelu(y).astype(y_ref.dtype)


def _s5_conv(ut_g, taps, g_t, c_t, a_pow, layer, batch):
    groups, lp, cols = ut_g.shape
    n = SSM_STATE
    kern = functools.partial(_s5_kernel, batch=batch)
    per_group = lambda *shape: pl.BlockSpec((1,) + shape, lambda g: (g,) + (0,) * len(shape))
    of_layer = lambda *shape: pl.BlockSpec((None, 1) + shape, lambda g: (layer, g) + (0,) * len(shape))
    return pl.pallas_call(
        kern,
        grid=(groups,),
        in_specs=[per_group(lp, cols), of_layer(SSM_GROUP, lp), of_layer(2 * n, lp),
                  of_layer(lp, 2 * n), of_layer(*a_pow.shape[2:])],
        out_specs=per_group(lp, cols),
        out_shape=jax.ShapeDtypeStruct((groups, lp, cols), BF16),
        scratch_shapes=[pltpu.VMEM((lp, lp), BF16)],
        compiler_params=_params("parallel"),
        name="s5_conv",
    )(ut_g, taps, g_t, c_t, a_pow)


def _s5_operators(lam_re, lam_im, log_step, b_re, b_im, c_re, c_im, d_skip, n_chunks):
    hp = lax.Precision.HIGHEST
    L, P, G, N = S5_CHUNK, SSM_GROUP, SSM_GROUPS, SSM_STATE
    lr = jnp.minimum(lam_re.astype(F32), -1e-4)
    li = lam_im.astype(F32)
    step = jnp.exp(log_step.astype(F32))[:, None]
    dr, di = lr * step, li * step
    tau = jnp.arange(L + 1, dtype=F32)
    mag = jnp.exp(dr[..., None] * tau)
    pw_re, pw_im = mag * jnp.cos(di[..., None] * tau), mag * jnp.sin(di[..., None] * tau)
    a_re, a_im = pw_re[..., 1], pw_im[..., 1]
    den = lr * lr + li * li
    f_re = ((a_re - 1.0) * lr + a_im * li) / den
    f_im = (a_im * lr - (a_re - 1.0) * li) / den
    br, bi = b_re.astype(F32), b_im.astype(F32)
    bb_re = f_re[..., None] * br - f_im[..., None] * bi
    bb_im = f_re[..., None] * bi + f_im[..., None] * br
    cr, ci = c_re.astype(F32), c_im.astype(F32)
    col = jnp.arange(P * L)
    pick_tau = (jnp.arange(L)[:, None] == L - 1 - col[None, :] % L).astype(F32)
    pick_p = (jnp.arange(P)[:, None] == col[None, :] // L).astype(F32)
    rv_re = jnp.einsum('gnt,tx->gnx', pw_re[..., :L], pick_tau, precision=hp)
    rv_im = jnp.einsum('gnt,tx->gnx', pw_im[..., :L], pick_tau, precision=hp)
    bx_re = jnp.einsum('gnp,px->gnx', bb_re, pick_p, precision=hp)
    bx_im = jnp.einsum('gnp,px->gnx', bb_im, pick_p, precision=hp)
    g_re, g_im = rv_re * bx_re - rv_im * bx_im, rv_re * bx_im + rv_im * bx_re
    g_t = jnp.concatenate([g_re, g_im], axis=1)
    taps = (jnp.einsum('gpn,gnx->gpx', cr, g_re, precision=hp)
            - jnp.einsum('gpn,gnx->gpx', ci, g_im, precision=hp))
    tap0 = (col[None, :] == jnp.arange(P)[:, None] * L + L - 1).astype(F32)
    taps = taps + d_skip.astype(F32).reshape(G, P, 1) * tap0
    cr, ci = cr[:, :, None, :], ci[:, :, None, :]
    nx_re = pw_re[..., 1:].transpose(0, 2, 1)[:, None]
    nx_im = pw_im[..., 1:].transpose(0, 2, 1)[:, None]
    c_t = jnp.concatenate([cr * nx_re - ci * nx_im, -(cr * nx_im + ci * nx_re)],
                          axis=-1).reshape(G, P * L, 2 * N)
    n_steps = max(1, (n_chunks - 1).bit_length())
    span = (L * 2.0 ** jnp.arange(n_steps, dtype=F32))[None, :, None]
    mag = jnp.exp(span * dr[:, None, :])
    a_pow = jnp.stack([mag * jnp.cos(span * di[:, None, :]), mag * jnp.sin(span * di[:, None, :])], axis=-1)
    return taps, g_t.astype(BF16), c_t.astype(BF16), a_pow


def _mix_out_kernel(x_ref, attn_ref, y_ref, gw_ref, gb_ref, wo_ref, g_ref, b_ref, o_ref):
    aw = attn_ref.shape[1]
    slab = x_ref.shape[0] // EPILOGUE_SLABS
    for r in range(EPILOGUE_SLABS):
        rows = slice(r * slab, (r + 1) * slab)
        yb = y_ref[rows, :]
        z = jnp.dot(yb, gw_ref[...], preferred_element_type=F32) + gb_ref[...]
        y = yb.astype(F32) * jax.nn.sigmoid(z)
        mixed = (jnp.dot(attn_ref[rows, :], wo_ref[:aw, :], preferred_element_type=F32)
                 + jnp.dot(y.astype(BF16), wo_ref[aw:, :], preferred_element_type=F32))
        o_ref[rows, :] = _layer_norm(ALPHA * x_ref[rows, :] + mixed, g_ref[...], b_ref[...])


def _mix_out(x, attn, y_ssm, glu_w, glu_b, w_out, g, b):
    t, d = x.shape
    tm = min(PROJ_TM, t)
    aw, sw = attn.shape[1], y_ssm.shape[1]
    row = lambda i: (i, 0)
    return pl.pallas_call(
        _mix_out_kernel,
        grid=(t // tm,),
        in_specs=[
            pl.BlockSpec((tm, d), row),
            pl.BlockSpec((tm, aw), row),
            pl.BlockSpec((tm, sw), row),
            _resident((sw, sw)),
            _resident((1, sw)),
            _resident((d, d)),
            _resident((1, d)),
            _resident((1, d)),
        ],
        out_specs=pl.BlockSpec((tm, d), row),
        out_shape=jax.ShapeDtypeStruct((t, d), F32),
        compiler_params=_params("parallel"),
        name="mix_out",
    )(x, attn, y_ssm, glu_w, glu_b, w_out, g, b)


def _mem_fold_kernel(m_ref, wq_ref, wk_ref, wv_ref, wo_ref, qk_ref, vo_ref):
    mb = m_ref[0].astype(BF16)
    k = jnp.dot(mb, wk_ref[...], preferred_element_type=F32).astype(BF16)
    v = jnp.dot(mb, wv_ref[...], preferred_element_type=F32).astype(BF16)
    qk = lax.dot_general(wq_ref[...], k, (((1,), (1,)), ((), ())),
                         preferred_element_type=F32)
    qk_ref[0] = (qk * (XATTN_HEAD_DIM ** -0.5)).astype(BF16)
    vo_ref[0] = jnp.dot(v, wo_ref[...], preferred_element_type=F32).astype(BF16)


def _mem_fold(mem, wq, wk, wv, wo):
    batch, n_mem, d = mem.shape
    hd = XATTN_HEAD_DIM
    col = lambda b, h: (0, h)
    return pl.pallas_call(
        _mem_fold_kernel,
        grid=(batch, XATTN_HEADS),
        in_specs=[pl.BlockSpec((1, n_mem, d), lambda b, h: (b, 0, 0)),
                  pl.BlockSpec((d, hd), col),
                  pl.BlockSpec((d, hd), col),
                  pl.BlockSpec((d, hd), col),
                  pl.BlockSpec((hd, d), lambda b, h: (h, 0))],
        out_specs=[pl.BlockSpec((1, d, n_mem), lambda b, h: (b, 0, h)),
                   pl.BlockSpec((1, n_mem, d), lambda b, h: (b, h, 0))],
        out_shape=[jax.ShapeDtypeStruct((batch, d, XATTN_HEADS * n_mem), BF16),
                   jax.ShapeDtypeStruct((batch, XATTN_HEADS * n_mem, d), BF16)],
        compiler_params=_params("parallel", "parallel"),
        name="mem_fold",
    )(mem, wq, wk, wv, wo)


def _xattn_ln_kernel(x_ref, qk_ref, vo_ref, g_ref, b_ref, o_ref, *, n_mem):
    x = x_ref[...]
    s = jnp.dot(x.astype(BF16), qk_ref[0], preferred_element_type=F32)
    probs = []
    for h in range(XATTN_HEADS):
        sh = s[:, h * n_mem:(h + 1) * n_mem]
        e = jnp.exp(sh - jnp.max(sh, axis=-1, keepdims=True))
        probs.append((e / jnp.sum(e, axis=-1, keepdims=True)).astype(BF16))
    out = jnp.dot(jnp.concatenate(probs, axis=1), vo_ref[0], preferred_element_type=F32)
    o_ref[...] = _layer_norm(ALPHA * x + out, g_ref[...], b_ref[...])


def _xattn_ln(x, qk, vo, g, b, seq):
    t, d = x.shape
    tm = min(PROJ_TM, seq)
    per_b = seq // tm
    hm = qk.shape[2]
    kern = functools.partial(_xattn_ln_kernel, n_mem=hm // XATTN_HEADS)
    return pl.pallas_call(
        kern,
        grid=(t // tm,),
        in_specs=[
            pl.BlockSpec((tm, d), lambda i: (i, 0)),
            pl.BlockSpec((1, d, hm), lambda i: (i // per_b, 0, 0)),
            pl.BlockSpec((1, hm, d), lambda i: (i // per_b, 0, 0)),
            _resident((1, d)),
            _resident((1, d)),
        ],
        out_specs=pl.BlockSpec((tm, d), lambda i: (i, 0)),
        out_shape=jax.ShapeDtypeStruct((t, d), F32),
        compiler_params=_params("parallel"),
        name="xattn_ln",
    )(x, qk, vo, g, b)


def kernel(x, mem, ffn1_w_gate, ffn1_w_up, ffn1_w_down, ln1_g, ln1_b, w_in, lambda_q1, lambda_k1, lambda_q2, lambda_k2, diff_norm_g, ssm_lambda_re, ssm_lambda_im, ssm_log_step, ssm_b_re, ssm_b_im, ssm_c_re, ssm_c_im, ssm_d, ssm_glu_w, ssm_glu_b, w_out, ln2_g, ln2_b, xattn_w_q, xattn_w_k, xattn_w_v, xattn_w_o, ln3_g, ln3_b, ffn2_w_gate, ffn2_w_up, ffn2_w_down, ln4_g, ln4_b):
    batch, seq, d = x.shape
    t = batch * seq
    aw = ATTN_WIDTH
    L, P, G = S5_CHUNK, SSM_GROUP, SSM_GROUPS
    n_chunks = seq // L
    xf = x.reshape(t, d)
    row = lambda a: a.reshape(1, -1).astype(F32)
    s5_ops = jax.vmap(functools.partial(_s5_operators, n_chunks=n_chunks))(
        ssm_lambda_re, ssm_lambda_im, ssm_log_step, ssm_b_re, ssm_b_im, ssm_c_re, ssm_c_im, ssm_d)

    for l in range(DEPTH):
        lam_init = 0.8 - 0.6 * math.exp(-0.3 * l)
        xf = _ffn_ln(xf, ffn1_w_gate[l].astype(BF16), ffn1_w_up[l].astype(BF16),
                     ffn1_w_down[l].astype(BF16), row(ln1_g[l]), row(ln1_b[l]))

        k_nat, qvut = _in_proj(xf, w_in[l].astype(BF16), w_in[l].T.astype(BF16))

        lam_p = jnp.stack([lambda_q1[l], lambda_k1[l], lambda_q2[l], lambda_k2[l]]).astype(F32)
        attn = _diff_attn(qvut, k_nat, lam_p, diff_norm_g[l].reshape(-1, 1).astype(F32), lam_init, batch, seq)

        ut_g = (qvut[2 * aw:].reshape(G * P, batch * n_chunks, L)
                .swapaxes(1, 2).reshape(G, P * L, batch * n_chunks))
        yt_g = _s5_conv(ut_g, *s5_ops, l, batch)
        y_ssm = yt_g.reshape(G * P, L, batch * n_chunks).transpose(2, 1, 0).reshape(t, G * P)

        xf = _mix_out(xf, attn, y_ssm, ssm_glu_w[l].astype(BF16),
                      row(ssm_glu_b[l]), w_out[l].astype(BF16), row(ln2_g[l]), row(ln2_b[l]))

        qk, vo = _mem_fold(mem, xattn_w_q[l].astype(BF16), xattn_w_k[l].astype(BF16),
                           xattn_w_v[l].astype(BF16), xattn_w_o[l].astype(BF16))
        xf = _xattn_ln(xf, qk, vo, row(ln3_g[l]), row(ln3_b[l]), seq)

        xf = _ffn_ln(xf, ffn2_w_gate[l].astype(BF16), ffn2_w_up[l].astype(BF16),
                     ffn2_w_down[l].astype(BF16), row(ln4_g[l]), row(ln4_b[l]))
    return xf.reshape(batch, seq, d)
```
